```python
import jax, jax.numpy as jnp
from jax import lax
import numpy as np

D_MODEL = 1024
BATCH = 32
SEQ = 256
DEPTH = 1
DEC_BATCH = 8
DEC_SEQ = 4096
PAST_LEN = 512

GRID_W = 64
N_HEADS = 4
QK_DIM = 64
V_DIM = 2 * QK_DIM
ATTN_W = N_HEADS * V_DIM
CONV_CH = D_MODEL // 2
DW_WIDTH = 31
D_FF = 2816
FFN_DW_WIDTH = 3
ROPE_THETA = 10000.0
EPS = 1e-6
Q_BLOCK = 128
N_MOD = 6
IN_COLS = 3 * ATTN_W + 2 * CONV_CH + 2 * D_MODEL

kernel_name = 'hybrid_diffattn_conformer_diffusion_step'


def rmsnorm(x, w):
    xf = x.astype(jnp.float32)
    y = xf * lax.rsqrt(jnp.mean(xf * xf, axis=-1, keepdims=True) + EPS)
    return (y * w.astype(jnp.float32)).astype(x.dtype)


def layernorm(x, g, b):
    xf = x.astype(jnp.float32)
    mu = jnp.mean(xf, axis=-1, keepdims=True)
    var = jnp.mean(jnp.square(xf - mu), axis=-1, keepdims=True)
    y = (xf - mu) * lax.rsqrt(var + EPS)
    return (y * g.astype(jnp.float32) + b.astype(jnp.float32)).astype(x.dtype)


def dwconv(x, w):
    return lax.conv_general_dilated(
        x, w[:, None, :].astype(x.dtype), window_strides=(1,), padding='SAME',
        dimension_numbers=('NWC', 'WIO', 'NWC'), feature_group_count=x.shape[-1])


def axial_rope_tables(T):
    rows = T // GRID_W
    row = jnp.repeat(jnp.arange(rows, dtype=jnp.float32), GRID_W)
    col = jnp.tile(jnp.arange(GRID_W, dtype=jnp.float32), rows)
    half = QK_DIM // 2
    freqs = ROPE_THETA ** (-jnp.arange(0, half, 2, dtype=jnp.float32) / half)
    ang = jnp.stack([row[:, None] * freqs, col[:, None] * freqs], axis=1)
    return jnp.cos(ang), jnp.sin(ang)


def apply_rope(x, cos, sin):
    shp = x.shape
    xr = x.reshape(shp[:-1] + (2, 2, QK_DIM // 4))
    x1 = xr[..., 0, :]
    x2 = xr[..., 1, :]
    c = cos[None, :, None, None].astype(x.dtype)
    s = sin[None, :, None, None].astype(x.dtype)
    out = jnp.stack([x1 * c - x2 * s, x1 * s + x2 * c], axis=-2)
    return out.reshape(shp)


def diff_attention(q, k, v, lam):
    B, T, H, _ = q.shape
    nblk = T // Q_BLOCK
    scale = QK_DIM ** -0.5
    k1 = k[..., :QK_DIM]
    k2 = k[..., QK_DIM:]
    qb = jnp.moveaxis(q.reshape(B, nblk, Q_BLOCK, H, 2 * QK_DIM), 1, 0)

    def block(qblk):
        s1 = jnp.einsum('bqhd,bkhd->bhqk', qblk[..., :QK_DIM], k1).astype(jnp.float32) * scale
        s2 = jnp.einsum('bqhd,bkhd->bhqk', qblk[..., QK_DIM:], k2).astype(jnp.float32) * scale
        a = jax.nn.softmax(s1, axis=-1) - lam * jax.nn.softmax(s2, axis=-1)
        return jnp.einsum('bhqk,bkhv->bqhv', a.astype(v.dtype), v)

    o = lax.map(block, qb)
    return jnp.moveaxis(o, 0, 1).reshape(B, T, H, V_DIM)


def token_mixer(h, lp, lam, lam_init, ctx_k, ctx_v, rope):
    B, T, _ = h.shape
    proj = h @ lp['w_in']
    q = proj[..., :ATTN_W].reshape(B, T, N_HEADS, 2, QK_DIM)
    k = proj[..., ATTN_W:2 * ATTN_W].reshape(B, T, N_HEADS, 2, QK_DIM)
    v = proj[..., 2 * ATTN_W:3 * ATTN_W].reshape(B, T, N_HEADS, V_DIM)
    u = proj[..., 3 * ATTN_W:3 * ATTN_W + 2 * CONV_CH]
    g = proj[..., 3 * ATTN_W + 2 * CONV_CH:]
    if rope is not None:
        q = apply_rope(q, rope[0], rope[1])
        k = apply_rope(k, rope[0], rope[1])
    q = q.reshape(B, T, N_HEADS, 2 * QK_DIM)
    k = k.reshape(B, T, N_HEADS, 2 * QK_DIM)
    if ctx_k is None:
        keys, vals = k, v
    else:
        keys = jnp.concatenate([k, ctx_k.astype(k.dtype)], axis=1)
        vals = jnp.concatenate([v, ctx_v.astype(v.dtype)], axis=1)
    o = diff_attention(q, keys, vals, lam)
    o = rmsnorm(o, lp['w_head_norm']) * (1.0 - lam_init)
    attn_out = o.reshape(B, T, ATTN_W) @ lp['w_attn_proj']
    glu = u[..., :CONV_CH] * jax.nn.sigmoid(u[..., CONV_CH:])
    cv = jax.nn.silu(layernorm(dwconv(glu, lp['w_conv_dw']), lp['conv_ln_g'], lp['conv_ln_b']))
    conv_out = cv @ lp['w_conv_proj']
    merged = jax.nn.sigmoid(g[..., :D_MODEL]) * attn_out + jax.nn.sigmoid(g[..., D_MODEL:]) * conv_out
    return merged @ lp['w_out'], k, v


def trunk_layer(x, mod, lp, lam, lam_init, ctx_k, ctx_v, rope):
    shift1, scale1, gate1, shift2, scale2, gate2 = jnp.split(mod, N_MOD, axis=-1)
    h = rmsnorm(x, lp['w_norm1']) * (1.0 + scale1) + shift1
    mix, k, v = token_mixer(h, lp, lam, lam_init, ctx_k, ctx_v, rope)
    x = x + gate1 * mix
    h = rmsnorm(x, lp['w_norm2']) * (1.0 + scale2) + shift2
    up = dwconv(h @ lp['w_up'], lp['w_ffn_dw'])
    ff = (jax.nn.silu(up[..., :D_FF]) * up[..., D_FF:]) @ lp['w_down']
    x = x + gate2 * ff
    return x, k, v


def setup_inputs(seed: int = 0) -> dict:
    key = jax.random.key(seed)
    ks = jax.random.split(key, 32)
    f = jnp.float32
    n = lambda i, shp, s: jax.random.normal(ks[i], shp, f) * s
    return {
        'x_prompt': n(0, (BATCH, SEQ, D_MODEL), 1.0),
        'x_sample': n(1, (DEC_BATCH, DEC_SEQ, D_MODEL), 1.0),
        'cache_k': n(2, (DEC_BATCH, DEPTH, PAST_LEN, N_HEADS, 2 * QK_DIM), 1.0),
        'cache_v': n(3, (DEC_BATCH, DEPTH, PAST_LEN, N_HEADS, V_DIM), 1.0),
        'c': n(4, (DEC_BATCH, D_MODEL), 1.0),
        'c_ctx': n(5, (D_MODEL,), 1.0),
        'w_ada': n(6, (DEPTH, D_MODEL, N_MOD * D_MODEL), 0.5 * D_MODEL ** -0.5),
        'b_ada': n(7, (DEPTH, N_MOD * D_MODEL), 0.01),
        'w_norm1': 1.0 + n(8, (DEPTH, D_MODEL), 0.01),
        'w_in': n(9, (DEPTH, D_MODEL, IN_COLS), D_MODEL ** -0.5),
        'lambda_q1': n(10, (DEPTH, QK_DIM), 0.1),
        'lambda_k1': n(11, (DEPTH, QK_DIM), 0.1),
        'lambda_q2': n(12, (DEPTH, QK_DIM), 0.1),
        'lambda_k2': n(13, (DEPTH, QK_DIM), 0.1),
        'w_head_norm': 1.0 + n(14, (DEPTH, V_DIM), 0.01),
        'w_attn_proj': n(15, (DEPTH, ATTN_W, D_MODEL), ATTN_W ** -0.5),
        'w_conv_dw': n(16, (DEPTH, DW_WIDTH, CONV_CH), DW_WIDTH ** -0.5),
        'conv_ln_g': 1.0 + n(17, (DEPTH, CONV_CH), 0.01),
        'conv_ln_b': n(18, (DEPTH, CONV_CH), 0.01),
        'w_conv_proj': n(19, (DEPTH, CONV_CH, D_MODEL), CONV_CH ** -0.5),
        'w_out': n(20, (DEPTH, D_MODEL, D_MODEL), D_MODEL ** -0.5),
        'w_norm2': 1.0 + n(21, (DEPTH, D_MODEL), 0.01),
        'w_up': n(22, (DEPTH, D_MODEL, 2 * D_FF), D_MODEL ** -0.5),
        'w_ffn_dw': n(23, (DEPTH, FFN_DW_WIDTH, 2 * D_FF), FFN_DW_WIDTH ** -0.5),
        'w_down': n(24, (DEPTH, D_FF, D_MODEL), D_FF ** -0.5),
        'w_final_norm': 1.0 + n(25, (D_MODEL,), 0.01),
    }


def reference(x_prompt, x_sample, cache_k, cache_v, c, c_ctx, w_ada, b_ada, w_norm1, w_in,
              lambda_q1, lambda_k1, lambda_q2, lambda_k2, w_head_norm, w_attn_proj,
              w_conv_dw, conv_ln_g, conv_ln_b, w_conv_proj, w_out, w_norm2, w_up,
              w_ffn_dw, w_down, w_final_norm):
    rope = axial_rope_tables(x_sample.shape[1])
    xp = x_prompt
    xs = x_sample
    new_k_list = []
    new_v_list = []
    for l in range(DEPTH):
        lp = dict(w_norm1=w_norm1[l], w_in=w_in[l], w_head_norm=w_head_norm[l],
                  w_attn_proj=w_attn_proj[l], w_conv_dw=w_conv_dw[l], conv_ln_g=conv_ln_g[l],
                  conv_ln_b=conv_ln_b[l], w_conv_proj=w_conv_proj[l], w_out=w_out[l],
                  w_norm2=w_norm2[l], w_up=w_up[l], w_ffn_dw=w_ffn_dw[l], w_down=w_down[l])
        lam_init = 0.8 - 0.6 * float(np.exp(-0.3 * l))
        lam = (jnp.exp(jnp.sum(lambda_q1[l].astype(jnp.float32) * lambda_k1[l].astype(jnp.float32)))
               - jnp.exp(jnp.sum(lambda_q2[l].astype(jnp.float32) * lambda_k2[l].astype(jnp.float32)))
               + lam_init)
        mod_ctx = (jax.nn.silu(c_ctx) @ w_ada[l] + b_ada[l])[None, None, :]
        mod_lat = (jax.nn.silu(c) @ w_ada[l] + b_ada[l])[:, None, :]
        xp, kp, vp = trunk_layer(xp, mod_ctx, lp, lam, lam_init, None, None, None)
        new_k_list.append(kp)
        new_v_list.append(vp)
        xs, _, _ = trunk_layer(xs, mod_lat, lp, lam, lam_init, cache_k[:, l], cache_v[:, l], rope)
    y_prompt = rmsnorm(xp, w_final_norm)
    y_sample = rmsnorm(xs, w_final_norm)
    new_k = jnp.stack(new_k_list, axis=1)
    new_v = jnp.stack(new_v_list, axis=1)
    return (y_prompt, y_sample, new_k, new_v)
```

```python
import functools

import jax
import jax.numpy as jnp
import numpy as np
from jax import lax
from jax.experimental import pallas as pl
from jax.experimental.pallas import tpu as pltpu

D_MODEL = 1024
N_HEADS = 4
QK_DIM = 64
V_DIM = 2 * QK_DIM
HEAD_W = 2 * QK_DIM
ATTN_W = N_HEADS * V_DIM
CONV_CH = D_MODEL // 2
DW_WIDTH = 31
D_FF = 2816
FFN_DW_WIDTH = 3
GRID_W = 64
ROPE_THETA = 10000.0
EPS = 1e-6
N_MOD = 6
LAM_INIT = 0.8 - 0.6 * float(np.exp(-0.3 * 0))
QK_SCALE = QK_DIM ** -0.5

HALO = 16
CONV_PAD = (DW_WIDTH - 1) // 2
VMEM_LIMIT = 56 * 1024 * 1024

F32 = jnp.float32
BF16 = jnp.bfloat16


def _sigmoid(x):
    return 1.0 / (1.0 + jnp.exp(-x))


def _const_spec(shape):
    nd = len(shape)
    return pl.BlockSpec(shape, lambda *_: (0,) * nd, pipeline_mode=pl.Buffered(1))


def _params(n_axes):
    return pltpu.CompilerParams(
        dimension_semantics=("arbitrary",) * n_axes, vmem_limit_bytes=VMEM_LIMIT)


def _mod_kernel(c_ref, w_ref, b_ref, o_ref):
    c = c_ref[...]
    s = c * _sigmoid(c)
    w = w_ref[...]
    s_hi = s.astype(BF16)
    s_lo = (s - s_hi.astype(F32)).astype(BF16)
    w_hi = w.astype(BF16)
    w_lo = (w - w_hi.astype(F32)).astype(BF16)
    acc = jnp.dot(s_hi, w_hi, preferred_element_type=F32)
    acc += jnp.dot(s_hi, w_lo, preferred_element_type=F32)
    acc += jnp.dot(s_lo, w_hi, preferred_element_type=F32)
    o_ref[...] = acc + b_ref[...]


def _mod_call(cc, w_ada, b_ada):
    rows = cc.shape[0]
    n_out = w_ada.shape[1]
    bn = 1536
    return pl.pallas_call(
        _mod_kernel,
        grid=(n_out // bn,),
        in_specs=[
            pl.BlockSpec((rows, D_MODEL), lambda j: (0, 0)),
            pl.BlockSpec((D_MODEL, bn), lambda j: (0, j)),
            pl.BlockSpec((1, bn), lambda j: (0, j)),
        ],
        out_specs=pl.BlockSpec((rows, bn), lambda j: (0, j)),
        out_shape=jax.ShapeDtypeStruct((rows, n_out), F32),
        compiler_params=_params(1),
        name="mod",
    )(cc, w_ada, b_ada)


def _rope_tables(seq):
    t = np.arange(seq)
    row = (t // GRID_W).astype(np.float32).astype(np.float64)
    col = (t % GRID_W).astype(np.float32).astype(np.float64)
    half = QK_DIM // 2
    freqs = ROPE_THETA ** (-np.arange(0, half, 2, dtype=np.float64) / half)
    ar = row[:, None] * freqs
    ac = col[:, None] * freqs
    cos = np.concatenate([np.cos(ar), np.cos(ar), np.cos(ac), np.cos(ac)], axis=1)
    sin = np.concatenate([-np.sin(ar), np.sin(ar), -np.sin(ac), np.sin(ac)], axis=1)
    cos = np.tile(cos, (1, HEAD_W // QK_DIM)).astype(np.float32)
    sin = np.tile(sin, (1, HEAD_W // QK_DIM)).astype(np.float32)
    return jnp.asarray(cos), jnp.asarray(sin)


def _rope(x, cos, sin):
    quarter = QK_DIM // 4
    lane = lax.broadcasted_iota(jnp.int32, (1, HEAD_W), 1)
    fwd = pltpu.roll(x, HEAD_W - quarter, 1)
    bwd = pltpu.roll(x, quarter, 1)
    partner = jnp.where((lane & quarter) == 0, fwd, bwd)
    return x * cos + partner * sin


def _in_proj_kernel(*refs, use_rope, emit_f32_kv):
    it = iter(refs)
    x_ref, mod_ref, wn_ref, w_ref = next(it), next(it), next(it), next(it)
    cos_ref = sin_ref = None
    if use_rope:
        cos_ref, sin_ref = next(it), next(it)
    q_ref, k_ref, v_ref, glu_ref, gate_ref = next(it), next(it), next(it), next(it), next(it)
    kf_ref = vf_ref = None
    if emit_f32_kv:
        kf_ref, vf_ref = next(it), next(it)

    x = x_ref[...]
    shift = mod_ref[:, 0:D_MODEL]
    scale = mod_ref[:, D_MODEL:2 * D_MODEL]
    ms = jnp.mean(x * x, axis=-1, keepdims=True)
    h = x * lax.rsqrt(ms + EPS) * wn_ref[...]
    h = h * (1.0 + scale) + shift
    hb = h.astype(BF16)

    def proj(c0, c1):
        return jnp.dot(hb, w_ref[:, c0:c1], preferred_element_type=F32)

    pq = proj(0, ATTN_W)
    pk = proj(ATTN_W, 2 * ATTN_W)
    if emit_f32_kv:
        kf_ref[...] = pk
    for hd in range(N_HEADS):
        sl = slice(hd * HEAD_W, (hd + 1) * HEAD_W)
        qh = pq[:, sl]
        kh = pk[:, sl]
        if use_rope:
            cos = cos_ref[...]
            sin = sin_ref[...]
            qh = _rope(qh, cos, sin)
            kh = _rope(kh, cos, sin)
        q_ref[:, sl] = (qh * QK_SCALE).astype(BF16)
        k_ref[:, sl] = kh.astype(BF16)

    pv = proj(2 * ATTN_W, 3 * ATTN_W)
    if emit_f32_kv:
        vf_ref[...] = pv
    v_ref[...] = pv.astype(BF16)

    u0 = 3 * ATTN_W
    pu = proj(u0, u0 + 2 * CONV_CH)
    glu_ref[...] = (pu[:, :CONV_CH] * _sigmoid(pu[:, CONV_CH:])).astype(BF16)

    g0 = u0 + 2 * CONV_CH
    for j in range(2):
        pg = proj(g0 + j * D_MODEL, g0 + (j + 1) * D_MODEL)
        gate_ref[:, j * D_MODEL:(j + 1) * D_MODEL] = _sigmoid(pg).astype(BF16)


def _in_proj_call(x2d, mod3, w_norm1, w_in_bf, seq, tm, per_seq_mod, use_rope, emit_f32_kv):
    n_tok = x2d.shape[0]
    tps = seq // tm
    in_cols = w_in_bf.shape[1]
    mod_idx = (lambda i: (i // tps, 0, 0)) if per_seq_mod else (lambda i: (0, 0, 0))
    in_specs = [
        pl.BlockSpec((tm, D_MODEL), lambda i: (i, 0)),
        pl.BlockSpec((None, 1, N_MOD * D_MODEL), mod_idx),
        _const_spec((1, D_MODEL)),
        _const_spec((D_MODEL, in_cols)),
    ]
    args = [x2d, mod3, w_norm1, w_in_bf]
    if use_rope:
        cos, sin = _rope_tables(seq)
        in_specs += [pl.BlockSpec((tm, HEAD_W), lambda i: (i % tps, 0))] * 2
        args += [cos, sin]
    tok_spec = lambda w: pl.BlockSpec((tm, w), lambda i: (i, 0))
    out_specs = [tok_spec(ATTN_W)] * 3 + [tok_spec(CONV_CH), tok_spec(2 * D_MODEL)]
    out_shape = [jax.ShapeDtypeStruct((n_tok, ATTN_W), BF16)] * 3 + [
        jax.ShapeDtypeStruct((n_tok, CONV_CH), BF16),
        jax.ShapeDtypeStruct((n_tok, 2 * D_MODEL), BF16)]
    if emit_f32_kv:
        out_specs += [tok_spec(ATTN_W)] * 2
        out_shape += [jax.ShapeDtypeStruct((n_tok, ATTN_W), F32)] * 2
    return pl.pallas_call(
        functools.partial(_in_proj_kernel, use_rope=use_rope, emit_f32_kv=emit_f32_kv),
        grid=(n_tok // tm,),
        in_specs=in_specs,
        out_specs=out_specs,
        out_shape=out_shape,
        compiler_params=_params(1),
        name="in_proj",
    )(*args)


def _attn_kernel(*refs, tq, tk, n_chunks, use_cache):
    it = iter(refs)
    q_ref, k_ref, v_ref = next(it), next(it), next(it)
    ck_ref = cv_ref = None
    if use_cache:
        ck_ref, cv_ref = next(it), next(it)
    lq1, lk1, lq2, lk2, whn_ref = next(it), next(it), next(it), next(it), next(it)
    o_ref = next(it)
    m_ref, l_ref, acc_ref = next(it), next(it), next(it)

    q = q_ref[...]
    lane = lax.broadcasted_iota(jnp.int32, (1, HEAD_W), 1)
    zero = jnp.zeros_like(q)
    qs = jnp.concatenate(
        [jnp.where(lane < QK_DIM, q, zero), jnp.where(lane >= QK_DIM, q, zero)], axis=0)

    m_ref[...] = jnp.full_like(m_ref, -jnp.inf)
    l_ref[...] = jnp.zeros_like(l_ref)
    acc_ref[...] = jnp.zeros_like(acc_ref)

    def update(kb, vb):
        s = lax.dot_general(qs, kb, (((1,), (1,)), ((), ())), preferred_element_type=F32)
        m_prev = m_ref[...]
        m_new = jnp.maximum(m_prev, jnp.max(s, axis=-1, keepdims=True))
        alpha = jnp.exp(m_prev - m_new)
        p = jnp.exp(s - m_new)
        l_ref[...] = alpha * l_ref[...] + jnp.sum(p, axis=-1, keepdims=True)
        acc_ref[...] = alpha * acc_ref[...] + jnp.dot(
            p.astype(BF16), vb, preferred_element_type=F32)
        m_ref[...] = m_new

    def body(j, carry):
        r0 = pl.multiple_of(j * tk, tk)
        update(k_ref[pl.ds(r0, tk), :], v_ref[pl.ds(r0, tk), :])
        return carry

    lax.fori_loop(0, n_chunks, body, 0)
    if use_cache:
        update(ck_ref[0].astype(BF16), cv_ref[0].astype(BF16))

    o1 = acc_ref[0:tq, :] / l_ref[0:tq, :]
    o2 = acc_ref[tq:2 * tq, :] / l_ref[tq:2 * tq, :]
    lam = (jnp.exp(jnp.sum(lq1[...] * lk1[...], axis=-1, keepdims=True))
           - jnp.exp(jnp.sum(lq2[...] * lk2[...], axis=-1, keepdims=True)) + LAM_INIT)
    o = o1 - lam * o2
    ms = jnp.mean(o * o, axis=-1, keepdims=True)
    o = o * lax.rsqrt(ms + EPS) * whn_ref[...] * (1.0 - LAM_INIT)
    o_ref[...] = o.astype(BF16)


def _attn_call(q, k, v, cache_k, cache_v, lam_params, w_head_norm, n_seq, seq, tq, tk):
    n_tok = q.shape[0]
    qps = seq // tq
    use_cache = cache_k is not None
    in_specs = [
        pl.BlockSpec((tq, HEAD_W), lambda b, h, i: (b * qps + i, h)),
        pl.BlockSpec((seq, HEAD_W), lambda b, h, i: (b, h)),
        pl.BlockSpec((seq, HEAD_W), lambda b, h, i: (b, h)),
    ]
    args = [q, k, v]
    if use_cache:
        past = cache_k.shape[1]
        in_specs += [pl.BlockSpec((1, past, HEAD_W), lambda b, h, i: (b, 0, h))] * 2
        args += [cache_k, cache_v]
    in_specs += [_const_spec((1, QK_DIM))] * 4 + [_const_spec((1, V_DIM))]
    args += list(lam_params) + [w_head_norm]
    return pl.pallas_call(
        functools.partial(_attn_kernel, tq=tq, tk=tk, n_chunks=seq // tk, use_cache=use_cache),
        grid=(n_seq, N_HEADS, qps),
        in_specs=in_specs,
        out_specs=pl.BlockSpec((tq, HEAD_W), lambda b, h, i: (b * qps + i, h)),
        out_shape=jax.ShapeDtypeStruct((n_tok, ATTN_W), BF16),
        scratch_shapes=[
            pltpu.VMEM((2 * tq, 1), F32),
            pltpu.VMEM((2 * tq, 1), F32),
            pltpu.VMEM((2 * tq, V_DIM), F32),
        ],
        compiler_params=_params(3),
        name="attn",
    )(*args)


def _halo_specs(tm, width, n_tok):
    r = tm // HALO
    last = n_tok // HALO - 1
    return [
        pl.BlockSpec((tm, width), lambda i: (i, 0)),
        pl.BlockSpec((HALO, width), lambda i: (jnp.maximum(i * r - 1, 0), 0)),
        pl.BlockSpec((HALO, width), lambda i: (jnp.minimum((i + 1) * r, last), 0)),
    ]


def _fill_ext(ext_ref, main_ref, prev_ref, next_ref, tm, tps):
    j = pl.program_id(0) % tps
    prev = prev_ref[...].astype(ext_ref.dtype)
    nxt = next_ref[...].astype(ext_ref.dtype)
    ext_ref[0:HALO, :] = jnp.where(j > 0, prev, jnp.zeros_like(prev))
    ext_ref[HALO:HALO + tm, :] = main_ref[...].astype(ext_ref.dtype)
    ext_ref[HALO + tm:, :] = jnp.where(j < tps - 1, nxt, jnp.zeros_like(nxt))


def _merge_kernel(o_ref, glu_ref, glu_prev, glu_next, gate_ref, x_ref, mod_ref,
                  wap_ref, wdw_ref, lng_ref, lnb_ref, wcp_ref, wout_ref, wn2_ref,
                  x1_ref, h2_ref, ext_ref, *, tm, tps, rows):
    _fill_ext(ext_ref, glu_ref, glu_prev, glu_next, tm, tps)
    base = HALO - CONV_PAD
    cvs = []
    for r0 in range(0, tm, rows):
        acc = ext_ref[pl.ds(base + r0, rows), :] * wdw_ref[0:1, :]
        for j in range(1, DW_WIDTH):
            acc += ext_ref[pl.ds(base + r0 + j, rows), :] * wdw_ref[j:j + 1, :]
        mu = jnp.mean(acc, axis=-1, keepdims=True)
        d = acc - mu
        var = jnp.mean(d * d, axis=-1, keepdims=True)
        y = d * lax.rsqrt(var + EPS) * lng_ref[...] + lnb_ref[...]
        cvs.append((y * _sigmoid(y)).astype(BF16))
    cv = jnp.concatenate(cvs, axis=0)
    conv_out = jnp.dot(cv, wcp_ref[...], preferred_element_type=F32)
    attn_out = jnp.dot(o_ref[...], wap_ref[...], preferred_element_type=F32)
    merged = (gate_ref[:, 0:D_MODEL].astype(F32) * attn_out
              + gate_ref[:, D_MODEL:2 * D_MODEL].astype(F32) * conv_out)
    mix = jnp.dot(merged.astype(BF16), wout_ref[...], preferred_element_type=F32)
    gate1 = mod_ref[:, 2 * D_MODEL:3 * D_MODEL]
    shift2 = mod_ref[:, 3 * D_MODEL:4 * D_MODEL]
    scale2 = mod_ref[:, 4 * D_MODEL:5 * D_MODEL]
    x1 = x_ref[...] + gate1 * mix
    x1_ref[...] = x1
    ms = jnp.mean(x1 * x1, axis=-1, keepdims=True)
    h2 = x1 * lax.rsqrt(ms + EPS) * wn2_ref[...]
    h2_ref[...] = (h2 * (1.0 + scale2) + shift2).astype(BF16)


def _merge_call(o_n, glu, gates, x2d, mod3, w_attn_proj, w_conv_dw, ln_g, ln_b,
                w_conv_proj, w_out, w_norm2, seq, tm, per_seq_mod):
    n_tok = x2d.shape[0]
    tps = seq // tm
    mod_idx = (lambda i: (i // tps, 0, 0)) if per_seq_mod else (lambda i: (0, 0, 0))
    tok_spec = lambda w: pl.BlockSpec((tm, w), lambda i: (i, 0))
    in_specs = (
        [tok_spec(ATTN_W)] + _halo_specs(tm, CONV_CH, n_tok)
        + [tok_spec(2 * D_MODEL), tok_spec(D_MODEL),
           pl.BlockSpec((None, 1, N_MOD * D_MODEL), mod_idx),
           _const_spec((ATTN_W, D_MODEL)), _const_spec((DW_WIDTH, CONV_CH)),
           _const_spec((1, CONV_CH)), _const_spec((1, CONV_CH)),
           _const_spec((CONV_CH, D_MODEL)), _const_spec((D_MODEL, D_MODEL)),
           _const_spec((1, D_MODEL))])
    return pl.pallas_call(
        functools.partial(_merge_kernel, tm=tm, tps=tps, rows=64),
        grid=(n_tok // tm,),
        in_specs=in_specs,
        out_specs=[tok_spec(D_MODEL), tok_spec(D_MODEL)],
        out_shape=[jax.ShapeDtypeStruct((n_tok, D_MODEL), F32),
                   jax.ShapeDtypeStruct((n_tok, D_MODEL), BF16)],
        scratch_shapes=[pltpu.VMEM((tm + 2 * HALO, CONV_CH), F32)],
        compiler_params=_params(1),
        name="merge",
    )(o_n, glu, glu, glu, gates, x2d, mod3, w_attn_proj, w_conv_dw, ln_g, ln_b,
      w_conv_proj, w_out, w_norm2)


def _ffn_kernel(h_ref, h_prev, h_next, x1_ref, mod_ref, wup_ref, wdw_ref, wdn_ref, wfn_ref,
                y_ref, lhs_ref, ua_ref, ub_ref, acc_ref, *, tm, tps, cn):
    _fill_ext(lhs_ref, h_ref, h_prev, h_next, tm, tps)
    lhs = lhs_ref[...]
    acc_ref[...] = jnp.zeros_like(acc_ref)

    def conv(u_ref, c0):
        out = u_ref[pl.ds(HALO - 1, tm), :] * wdw_ref[0:1, c0:c0 + cn]
        out += u_ref[pl.ds(HALO, tm), :] * wdw_ref[1:2, c0:c0 + cn]
        out += u_ref[pl.ds(HALO + 1, tm), :] * wdw_ref[2:3, c0:c0 + cn]
        return out

    for c in range(D_FF // cn):
        a0 = c * cn
        b0 = D_FF + c * cn
        ua_ref[...] = jnp.dot(lhs, wup_ref[:, a0:a0 + cn], preferred_element_type=F32)
        ub_ref[...] = jnp.dot(lhs, wup_ref[:, b0:b0 + cn], preferred_element_type=F32)
        a = conv(ua_ref, a0)
        b = conv(ub_ref, b0)
        act = (a * _sigmoid(a) * b).astype(BF16)
        acc_ref[...] += jnp.dot(act, wdn_ref[a0:a0 + cn, :], preferred_element_type=F32)

    gate2 = mod_ref[:, 5 * D_MODEL:6 * D_MODEL]
    y = x1_ref[...] + gate2 * acc_ref[...]
    ms = jnp.mean(y * y, axis=-1, keepdims=True)
    y_ref[...] = y * lax.rsqrt(ms + EPS) * wfn_ref[...]


def _ffn_call(h2, x1, mod3, w_up, w_ffn_dw, w_down, w_final_norm, seq, tm, per_seq_mod):
    n_tok = x1.shape[0]
    tps = seq // tm
    cn = 256
    mod_idx = (lambda i: (i // tps, 0, 0)) if per_seq_mod else (lambda i: (0, 0, 0))
    tok_spec = lambda w: pl.BlockSpec((tm, w), lambda i: (i, 0))
    in_specs = (
        _halo_specs(tm, D_MODEL, n_tok)
        + [tok_spec(D_MODEL), pl.BlockSpec((None, 1, N_MOD * D_MODEL), mod_idx),
           _const_spec((D_MODEL, 2 * D_FF)), _const_spec((FFN_DW_WIDTH, 2 * D_FF)),
           _const_spec((D_FF, D_MODEL)), _const_spec((1, D_MODEL))])
    return pl.pallas_call(
        functools.partial(_ffn_kernel, tm=tm, tps=tps, cn=cn),
        grid=(n_tok // tm,),
        in_specs=in_specs,
        out_specs=tok_spec(D_MODEL),
        out_shape=jax.ShapeDtypeStruct((n_tok, D_MODEL), F32),
        scratch_shapes=[
            pltpu.VMEM((tm + 2 * HALO, D_MODEL), BF16),
            pltpu.VMEM((tm + 2 * HALO, cn), F32),
            pltpu.VMEM((tm + 2 * HALO, cn), F32),
            pltpu.VMEM((tm, D_MODEL), F32),
        ],
        compiler_params=_params(1),
        name="ffn",
    )(h2, h2, h2, x1, mod3, w_up, w_ffn_dw, w_down, w_final_norm)


def _trunk_group(x, mod3, per_seq_mod, use_rope, cache, lp, tm, tq, tk):
    n_seq, seq, _ = x.shape
    x2d = x.reshape(n_seq * seq, D_MODEL)
    emit_f32_kv = cache is None
    outs = _in_proj_call(x2d, mod3, lp["w_norm1"], lp["w_in"], seq, tm,
                         per_seq_mod, use_rope, emit_f32_kv)
    q, k, v, glu, gates = outs[:5]
    cache_k = cache_v = None
    if cache is not None:
        cache_k, cache_v = cache
    o_n = _attn_call(q, k, v, cache_k, cache_v, lp["lam"], lp["w_head_norm"],
                     n_seq, seq, tq, tk)
    x1, h2 = _merge_call(o_n, glu, gates, x2d, mod3, lp["w_attn_proj"], lp["w_conv_dw"],
                         lp["conv_ln_g"], lp["conv_ln_b"], lp["w_conv_proj"], lp["w_out"],
                         lp["w_norm2"], seq, tm, per_seq_mod)
    y = _ffn_call(h2, x1, mod3, lp["w_up"], lp["w_ffn_dw"], lp["w_down"],
                  lp["w_final_norm"], seq, tm, per_seq_mod)
    return y.reshape(n_seq, seq, D_MODEL), outs[5:]


def kernel(x_prompt, x_sample, cache_k, cache_v, c, c_ctx, w_ada, b_ada, w_norm1, w_in, lambda_q1, lambda_k1, lambda_q2, lambda_k2, w_head_norm, w_attn_proj, w_conv_dw, conv_ln_g, conv_ln_b, w_conv_proj, w_out, w_norm2, w_up, w_ffn_dw, w_down, w_final_norm):
    assert w_in.shape[0] == 1, "single trunk layer"
    n_dec = x_sample.shape[0]
    n_ctx, seq_ctx = x_prompt.shape[0], x_prompt.shape[1]
    past = cache_k.shape[2]

    mod_rows = 16
    cc = jnp.concatenate(
        [c, c_ctx[None, :], jnp.zeros((mod_rows - n_dec - 1, D_MODEL), F32)], axis=0)
    mod = _mod_call(cc, w_ada[0], b_ada)
    mod_lat = mod[:n_dec].reshape(n_dec, 1, N_MOD * D_MODEL)
    mod_ctx = mod[n_dec:n_dec + 1].reshape(1, 1, N_MOD * D_MODEL)

    lp = dict(
        w_norm1=w_norm1, w_in=w_in[0].astype(BF16),
        lam=(lambda_q1, lambda_k1, lambda_q2, lambda_k2), w_head_norm=w_head_norm,
        w_attn_proj=w_attn_proj[0].astype(BF16), w_conv_dw=w_conv_dw[0],
        conv_ln_g=conv_ln_g, conv_ln_b=conv_ln_b,
        w_conv_proj=w_conv_proj[0].astype(BF16), w_out=w_out[0].astype(BF16),
        w_norm2=w_norm2, w_up=w_up[0].astype(BF16), w_ffn_dw=w_ffn_dw[0],
        w_down=w_down[0].astype(BF16), w_final_norm=w_final_norm[None, :])

    y_prompt, (kf, vf) = _trunk_group(
        x_prompt, mod_ctx, False, False, None, lp, tm=seq_ctx, tq=seq_ctx, tk=seq_ctx)
    cache = (cache_k[:, 0].reshape(n_dec, past, ATTN_W),
             cache_v[:, 0].reshape(n_dec, past, ATTN_W))
    y_sample, _ = _trunk_group(
        x_sample, mod_lat, True, True, cache, lp, tm=512, tq=256, tk=512)
    new_k = kf.reshape(n_ctx, 1, seq_ctx, N_HEADS, HEAD_W)
    new_v = vf.reshape(n_ctx, 1, seq_ctx, N_HEADS, V_DIM)
    return (y_prompt, y_sample, new_k, new_v)
```

```python
import functools

import jax
import jax.numpy as jnp
import numpy as np
from jax import lax
from jax.experimental import pallas as pl
from jax.experimental.pallas import tpu as pltpu

D_MODEL = 1024
N_HEADS = 4
QK_DIM = 64
V_DIM = 2 * QK_DIM
HEAD_W = 2 * QK_DIM
ATTN_W = N_HEADS * V_DIM
CONV_CH = D_MODEL // 2
DW_WIDTH = 31
D_FF = 2816
FFN_DW_WIDTH = 3
GRID_W = 64
ROPE_THETA = 10000.0
EPS = 1e-6
N_MOD = 6
LAM_INIT = 0.8 - 0.6 * float(np.exp(-0.3 * 0))
QK_SCALE = QK_DIM ** -0.5 * float(np.log2(np.e))

HALO = 16
CONV_PAD = (DW_WIDTH - 1) // 2
VMEM_LIMIT = 56 * 1024 * 1024

F32 = jnp.float32
BF16 = jnp.bfloat16


def _sigmoid(x):
    return 1.0 / (1.0 + jnp.exp(-x))


def _const_spec(shape):
    nd = len(shape)
    return pl.BlockSpec(shape, lambda *_: (0,) * nd, pipeline_mode=pl.Buffered(1))


def _params(n_axes):
    return pltpu.CompilerParams(
        dimension_semantics=("arbitrary",) * n_axes, vmem_limit_bytes=VMEM_LIMIT)


def _mod_kernel(c_ref, w_ref, b_ref, o_ref):
    c = c_ref[...]
    s = c * _sigmoid(c)
    w = w_ref[...]
    s_hi = s.astype(BF16)
    s_lo = (s - s_hi.astype(F32)).astype(BF16)
    w_hi = w.astype(BF16)
    w_lo = (w - w_hi.astype(F32)).astype(BF16)
    acc = jnp.dot(s_hi, w_hi, preferred_element_type=F32)
    acc += jnp.dot(s_hi, w_lo, preferred_element_type=F32)
    acc += jnp.dot(s_lo, w_hi, preferred_element_type=F32)
    o_ref[...] = acc + b_ref[...]


def _mod_call(cc, w_ada, b_ada):
    rows = cc.shape[0]
    n_out = w_ada.shape[1]
    bn = 1536
    return pl.pallas_call(
        _mod_kernel,
        grid=(n_out // bn,),
        in_specs=[
            pl.BlockSpec((rows, D_MODEL), lambda j: (0, 0)),
            pl.BlockSpec((D_MODEL, bn), lambda j: (0, j)),
            pl.BlockSpec((1, bn), lambda j: (0, j)),
        ],
        out_specs=pl.BlockSpec((rows, bn), lambda j: (0, j)),
        out_shape=jax.ShapeDtypeStruct((rows, n_out), F32),
        compiler_params=_params(1),
        name="mod",
    )(cc, w_ada, b_ada)


def _rope_tables(seq):
    t = np.arange(seq)
    row = (t // GRID_W).astype(np.float32).astype(np.float64)
    col = (t % GRID_W).astype(np.float32).astype(np.float64)
    half = QK_DIM // 2
    freqs = ROPE_THETA ** (-np.arange(0, half, 2, dtype=np.float64) / half)
    ar = row[:, None] * freqs
    ac = col[:, None] * freqs
    cos = np.concatenate([np.cos(ar), np.cos(ar), np.cos(ac), np.cos(ac)], axis=1)
    sin = np.concatenate([-np.sin(ar), np.sin(ar), -np.sin(ac), np.sin(ac)], axis=1)
    cos = np.tile(cos, (1, HEAD_W // QK_DIM)).astype(np.float32)
    sin = np.tile(sin, (1, HEAD_W // QK_DIM)).astype(np.float32)
    return jnp.asarray(cos), jnp.asarray(sin)


def _rope(x, cos, sin):
    quarter = QK_DIM // 4
    lane = lax.broadcasted_iota(jnp.int32, (1, HEAD_W), 1)
    fwd = pltpu.roll(x, HEAD_W - quarter, 1)
    bwd = pltpu.roll(x, quarter, 1)
    partner = jnp.where((lane & quarter) == 0, fwd, bwd)
    return x * cos + partner * sin


def _in_proj_kernel(*refs, use_rope, emit_f32_kv):
    it = iter(refs)
    x_ref, mod_ref, wn_ref, w_ref = next(it), next(it), next(it), next(it)
    cos_ref = sin_ref = None
    if use_rope:
        cos_ref, sin_ref = next(it), next(it)
    qt_ref, k_ref, vt_ref, glu_ref, gate_ref = next(it), next(it), next(it), next(it), next(it)
    kf_ref = vf_ref = None
    if emit_f32_kv:
        kf_ref, vf_ref = next(it), next(it)

    x = x_ref[...]
    shift = mod_ref[:, 0:D_MODEL]
    scale = mod_ref[:, D_MODEL:2 * D_MODEL]
    ms = jnp.mean(x * x, axis=-1, keepdims=True)
    h = x * lax.rsqrt(ms + EPS) * wn_ref[...]
    h = h * (1.0 + scale) + shift
    hb = h.astype(BF16)

    def proj(c0, c1):
        return jnp.dot(hb, w_ref[:, c0:c1], preferred_element_type=F32)

    pq = proj(0, ATTN_W)
    pk = proj(ATTN_W, 2 * ATTN_W)
    if emit_f32_kv:
        kf_ref[...] = pk
    for hd in range(N_HEADS):
        sl = slice(hd * HEAD_W, (hd + 1) * HEAD_W)
        qh = pq[:, sl]
        kh = pk[:, sl]
        if use_rope:
            cos = cos_ref[...]
            sin = sin_ref[...]
            qh = _rope(qh, cos, sin)
            kh = _rope(kh, cos, sin)
        qt_ref[sl, :] = (qh * QK_SCALE).T.astype(BF16)
        k_ref[:, sl] = kh.astype(BF16)

    pv = proj(2 * ATTN_W, 3 * ATTN_W)
    if emit_f32_kv:
        vf_ref[...] = pv
    for hd in range(N_HEADS):
        sl = slice(hd * HEAD_W, (hd + 1) * HEAD_W)
        vt_ref[sl, :] = pv[:, sl].T.astype(BF16)

    u0 = 3 * ATTN_W
    pu = proj(u0, u0 + 2 * CONV_CH)
    glu_ref[...] = (pu[:, :CONV_CH] * _sigmoid(pu[:, CONV_CH:])).astype(BF16)

    g0 = u0 + 2 * CONV_CH
    for j in range(2):
        pg = proj(g0 + j * D_MODEL, g0 + (j + 1) * D_MODEL)
        gate_ref[:, j * D_MODEL:(j + 1) * D_MODEL] = _sigmoid(pg).astype(BF16)


def _in_proj_call(x2d, mod3, w_norm1, w_in_bf, seq, tm, per_seq_mod, use_rope, emit_f32_kv):
    n_tok = x2d.shape[0]
    tps = seq // tm
    in_cols = w_in_bf.shape[1]
    mod_idx = (lambda i: (i // tps, 0, 0)) if per_seq_mod else (lambda i: (0, 0, 0))
    in_specs = [
        pl.BlockSpec((tm, D_MODEL), lambda i: (i, 0)),
        pl.BlockSpec((None, 1, N_MOD * D_MODEL), mod_idx),
        _const_spec((1, D_MODEL)),
        _const_spec((D_MODEL, in_cols)),
    ]
    args = [x2d, mod3, w_norm1, w_in_bf]
    if use_rope:
        cos, sin = _rope_tables(seq)
        in_specs += [pl.BlockSpec((tm, HEAD_W), lambda i: (i % tps, 0))] * 2
        args += [cos, sin]
    tok_spec = lambda w: pl.BlockSpec((tm, w), lambda i: (i, 0))
    tr_spec = pl.BlockSpec((ATTN_W, tm), lambda i: (0, i))
    tr_shape = jax.ShapeDtypeStruct((ATTN_W, n_tok), BF16)
    out_specs = [tr_spec, tok_spec(ATTN_W), tr_spec, tok_spec(CONV_CH), tok_spec(2 * D_MODEL)]
    out_shape = [tr_shape, jax.ShapeDtypeStruct((n_tok, ATTN_W), BF16), tr_shape,
                 jax.ShapeDtypeStruct((n_tok, CONV_CH), BF16),
                 jax.ShapeDtypeStruct((n_tok, 2 * D_MODEL), BF16)]
    if emit_f32_kv:
        out_specs += [tok_spec(ATTN_W)] * 2
        out_shape += [jax.ShapeDtypeStruct((n_tok, ATTN_W), F32)] * 2
    return pl.pallas_call(
        functools.partial(_in_proj_kernel, use_rope=use_rope, emit_f32_kv=emit_f32_kv),
        grid=(n_tok // tm,),
        in_specs=in_specs,
        out_specs=out_specs,
        out_shape=out_shape,
        compiler_params=_params(1),
        name="in_proj",
    )(*args)


def _attn_kernel(*refs, tq, tk, n_chunks, use_cache):
    it = iter(refs)
    qt_ref, k_ref, vt_ref = next(it), next(it), next(it)
    ck_ref = cv_ref = None
    if use_cache:
        ck_ref, cv_ref = next(it), next(it)
    lq1, lk1, lq2, lk2, whn_ref = next(it), next(it), next(it), next(it), next(it)
    o_ref = next(it)

    qt = qt_ref[...]
    zero = jnp.zeros((QK_DIM, tq), BF16)
    qs = jnp.concatenate(
        [jnp.concatenate([qt[0:QK_DIM, :], zero], axis=0),
         jnp.concatenate([zero, qt[QK_DIM:HEAD_W, :]], axis=0)], axis=1)

    def update(state, kb, vtb):
        m_prev, l_prev, acc = state
        s = jnp.dot(kb, qs, preferred_element_type=F32)
        m_new = jnp.maximum(m_prev, jnp.max(s, axis=0, keepdims=True))
        alpha = jnp.exp2(m_prev - m_new)
        p = jnp.exp2(s - m_new)
        l_new = alpha * l_prev + jnp.sum(p, axis=0, keepdims=True)
        acc = alpha * acc + jnp.dot(vtb, p.astype(BF16), preferred_element_type=F32)
        return m_new, l_new, acc

    state = (jnp.full((1, 2 * tq), -jnp.inf, F32), jnp.zeros((1, 2 * tq), F32),
             jnp.zeros((V_DIM, 2 * tq), F32))
    for j in range(n_chunks):
        state = update(state, k_ref[j * tk:(j + 1) * tk, :], vt_ref[:, j * tk:(j + 1) * tk])
    if use_cache:
        state = update(state, ck_ref[0].astype(BF16), cv_ref[0].T.astype(BF16))
    _, l, acc = state

    ot = acc / l
    lam = (jnp.exp(jnp.sum(lq1[...] * lk1[...], axis=-1, keepdims=True))
           - jnp.exp(jnp.sum(lq2[...] * lk2[...], axis=-1, keepdims=True)) + LAM_INIT)
    o = (ot[:, 0:tq] - lam * ot[:, tq:2 * tq]).T
    ms = jnp.mean(o * o, axis=-1, keepdims=True)
    o = o * lax.rsqrt(ms + EPS) * whn_ref[...] * (1.0 - LAM_INIT)
    o_ref[...] = o.astype(BF16)


def _attn_call(qt, k, vt, cache_k, cache_v, lam_params, w_head_norm, n_seq, seq, tq, tk):
    n_tok = k.shape[0]
    qps = seq // tq
    use_cache = cache_k is not None
    in_specs = [
        pl.BlockSpec((HEAD_W, tq), lambda b, h, i: (h, b * qps + i)),
        pl.BlockSpec((seq, HEAD_W), lambda b, h, i: (b, h)),
        pl.BlockSpec((HEAD_W, seq), lambda b, h, i: (h, b)),
    ]
    args = [qt, k, vt]
    if use_cache:
        past = cache_k.shape[1]
        in_specs += [pl.BlockSpec((1, past, HEAD_W), lambda b, h, i: (b, 0, h))] * 2
        args += [cache_k, cache_v]
    in_specs += [_const_spec((1, QK_DIM))] * 4 + [_const_spec((1, V_DIM))]
    args += list(lam_params) + [w_head_norm]
    return pl.pallas_call(
        functools.partial(_attn_kernel, tq=tq, tk=tk, n_chunks=seq // tk, use_cache=use_cache),
        grid=(n_seq, N_HEADS, qps),
        in_specs=in_specs,
        out_specs=pl.BlockSpec((tq, HEAD_W), lambda b, h, i: (b * qps + i, h)),
        out_shape=jax.ShapeDtypeStruct((n_tok, ATTN_W), BF16),
        compiler_params=_params(3),
        name="attn",
    )(*args)


def _halo_specs(tm, width, n_tok):
    r = tm // HALO
    last = n_tok // HALO - 1
    return [
        pl.BlockSpec((tm, width), lambda i: (i, 0)),
        pl.BlockSpec((HALO, width), lambda i: (jnp.maximum(i * r - 1, 0), 0)),
        pl.BlockSpec((HALO, width), lambda i: (jnp.minimum((i + 1) * r, last), 0)),
    ]


def _fill_ext(ext_ref, main_ref, prev_ref, next_ref, tm, tps):
    j = pl.program_id(0) % tps
    prev = prev_ref[...].astype(ext_ref.dtype)
    nxt = next_ref[...].astype(ext_ref.dtype)
    ext_ref[0:HALO, :] = jnp.where(j > 0, prev, jnp.zeros_like(prev))
    ext_ref[HALO:HALO + tm, :] = main_ref[...].astype(ext_ref.dtype)
    ext_ref[HALO + tm:, :] = jnp.where(j < tps - 1, nxt, jnp.zeros_like(nxt))


def _merge_kernel(o_ref, glu_ref, glu_prev, glu_next, gate_ref, x_ref, mod_ref,
                  wap_ref, wdw_ref, lng_ref, lnb_ref, wcp_ref, wout_ref, wn2_ref,
                  x1_ref, h2_ref, ext_ref, *, tm, tps, rows):
    _fill_ext(ext_ref, glu_ref, glu_prev, glu_next, tm, tps)
    base = HALO - CONV_PAD
    cvs = []
    for r0 in range(0, tm, rows):
        acc = ext_ref[pl.ds(base + r0, rows), :] * wdw_ref[0:1, :]
        for j in range(1, DW_WIDTH):
            acc += ext_ref[pl.ds(base + r0 + j, rows), :] * wdw_ref[j:j + 1, :]
        mu = jnp.mean(acc, axis=-1, keepdims=True)
        d = acc - mu
        var = jnp.mean(d * d, axis=-1, keepdims=True)
        y = d * lax.rsqrt(var + EPS) * lng_ref[...] + lnb_ref[...]
        cvs.append((y * _sigmoid(y)).astype(BF16))
    cv = jnp.concatenate(cvs, axis=0)
    conv_out = jnp.dot(cv, wcp_ref[...], preferred_element_type=F32)
    attn_out = jnp.dot(o_ref[...], wap_ref[...], preferred_element_type=F32)
    merged = (gate_ref[:, 0:D_MODEL].astype(F32) * attn_out
              + gate_ref[:, D_MODEL:2 * D_MODEL].astype(F32) * conv_out)
    mix = jnp.dot(merged.astype(BF16), wout_ref[...], preferred_element_type=F32)
    gate1 = mod_ref[:, 2 * D_MODEL:3 * D_MODEL]
    shift2 = mod_ref[:, 3 * D_MODEL:4 * D_MODEL]
    scale2 = mod_ref[:, 4 * D_MODEL:5 * D_MODEL]
    x1 = x_ref[...] + gate1 * mix
    x1_ref[...] = x1
    ms = jnp.mean(x1 * x1, axis=-1, keepdims=True)
    h2 = x1 * lax.rsqrt(ms + EPS) * wn2_ref[...]
    h2_ref[...] = (h2 * (1.0 + scale2) + shift2).astype(BF16)


def _merge_call(o_n, glu, gates, x2d, mod3, w_attn_proj, w_conv_dw, ln_g, ln_b,
                w_conv_proj, w_out, w_norm2, seq, tm, per_seq_mod):
    n_tok = x2d.shape[0]
    tps = seq // tm
    mod_idx = (lambda i: (i // tps, 0, 0)) if per_seq_mod else (lambda i: (0, 0, 0))
    tok_spec = lambda w: pl.BlockSpec((tm, w), lambda i: (i, 0))
    in_specs = (
        [tok_spec(ATTN_W)] + _halo_specs(tm, CONV_CH, n_tok)
        + [tok_spec(2 * D_MODEL), tok_spec(D_MODEL),
           pl.BlockSpec((None, 1, N_MOD * D_MODEL), mod_idx),
           _const_spec((ATTN_W, D_MODEL)), _const_spec((DW_WIDTH, CONV_CH)),
           _const_spec((1, CONV_CH)), _const_spec((1, CONV_CH)),
           _const_spec((CONV_CH, D_MODEL)), _const_spec((D_MODEL, D_MODEL)),
           _const_spec((1, D_MODEL))])
    return pl.pallas_call(
        functools.partial(_merge_kernel, tm=tm, tps=tps, rows=64),
        grid=(n_tok // tm,),
        in_specs=in_specs,
        out_specs=[tok_spec(D_MODEL), tok_spec(D_MODEL)],
        out_shape=[jax.ShapeDtypeStruct((n_tok, D_MODEL), F32),
                   jax.ShapeDtypeStruct((n_tok, D_MODEL), BF16)],
        scratch_shapes=[pltpu.VMEM((tm + 2 * HALO, CONV_CH), F32)],
        compiler_params=_params(1),
        name="merge",
    )(o_n, glu, glu, glu, gates, x2d, mod3, w_attn_proj, w_conv_dw, ln_g, ln_b,
      w_conv_proj, w_out, w_norm2)


def _ffn_kernel(h_ref, h_prev, h_next, x1_ref, mod_ref, wup_ref, wdw_ref, wdn_ref, wfn_ref,
                y_ref, lhs_ref, ua_ref, ub_ref, acc_ref, *, tm, tps, cn):
    _fill_ext(lhs_ref, h_ref, h_prev, h_next, tm, tps)
    lhs = lhs_ref[...]
    acc_ref[...] = jnp.zeros_like(acc_ref)

    def conv(u_ref, c0):
        out = u_ref[pl.ds(HALO - 1, tm), :] * wdw_ref[0:1, c0:c0 + cn]
        out += u_ref[pl.ds(HALO, tm), :] * wdw_ref[1:2, c0:c0 + cn]
        out += u_ref[pl.ds(HALO + 1, tm), :] * wdw_ref[2:3, c0:c0 + cn]
        return out

    for c in range(D_FF // cn):
        a0 = c * cn
        b0 = D_FF + c * cn
        ua_ref[...] = jnp.dot(lhs, wup_ref[:, a0:a0 + cn], preferred_element_type=F32)
        ub_ref[...] = jnp.dot(lhs, wup_ref[:, b0:b0 + cn], preferred_element_type=F32)
        a = conv(ua_ref, a0)
        b = conv(ub_ref, b0)
        act = (a * _sigmoid(a) * b).astype(BF16)
        acc_ref[...] += jnp.dot(act, wdn_ref[a0:a0 + cn, :], preferred_element_type=F32)

    gate2 = mod_ref[:, 5 * D_MODEL:6 * D_MODEL]
    y = x1_ref[...] + gate2 * acc_ref[...]
    ms = jnp.mean(y * y, axis=-1, keepdims=True)
    y_ref[...] = y * lax.rsqrt(ms + EPS) * wfn_ref[...]


def _ffn_call(h2, x1, mod3, w_up, w_ffn_dw, w_down, w_final_norm, seq, tm, per_seq_mod):
    n_tok = x1.shape[0]
    tps = seq // tm
    cn = 256
    mod_idx = (lambda i: (i // tps, 0, 0)) if per_seq_mod else (lambda i: (0, 0, 0))
    tok_spec = lambda w: pl.BlockSpec((tm, w), lambda i: (i, 0))
    in_specs = (
        _halo_specs(tm, D_MODEL, n_tok)
        + [tok_spec(D_MODEL), pl.BlockSpec((None, 1, N_MOD * D_MODEL), mod_idx),
           _const_spec((D_MODEL, 2 * D_FF)), _const_spec((FFN_DW_WIDTH, 2 * D_FF)),
           _const_spec((D_FF, D_MODEL)), _const_spec((1, D_MODEL))])
    return pl.pallas_call(
        functools.partial(_ffn_kernel, tm=tm, tps=tps, cn=cn),
        grid=(n_tok // tm,),
        in_specs=in_specs,
        out_specs=tok_spec(D_MODEL),
        out_shape=jax.ShapeDtypeStruct((n_tok, D_MODEL), F32),
        scratch_shapes=[
            pltpu.VMEM((tm + 2 * HALO, D_MODEL), BF16),
            pltpu.VMEM((tm + 2 * HALO, cn), F32),
            pltpu.VMEM((tm + 2 * HALO, cn), F32),
            pltpu.VMEM((tm, D_MODEL), F32),
        ],
        compiler_params=_params(1),
        name="ffn",
    )(h2, h2, h2, x1, mod3, w_up, w_ffn_dw, w_down, w_final_norm)


def _trunk_group(x, mod3, per_seq_mod, use_rope, cache, lp, tm, tq, tk):
    n_seq, seq, _ = x.shape
    x2d = x.reshape(n_seq * seq, D_MODEL)
    emit_f32_kv = cache is None
    outs = _in_proj_call(x2d, mod3, lp["w_norm1"], lp["w_in"], seq, tm,
                         per_seq_mod, use_rope, emit_f32_kv)
    q, k, v, glu, gates = outs[:5]
    cache_k = cache_v = None
    if cache is not None:
        cache_k, cache_v = cache
    o_n = _attn_call(q, k, v, cache_k, cache_v, lp["lam"], lp["w_head_norm"],
                     n_seq, seq, tq, tk)
    x1, h2 = _merge_call(o_n, glu, gates, x2d, mod3, lp["w_attn_proj"], lp["w_conv_dw"],
                         lp["conv_ln_g"], lp["conv_ln_b"], lp["w_conv_proj"], lp["w_out"],
                         lp["w_norm2"], seq, tm, per_seq_mod)
    y = _ffn_call(h2, x1, mod3, lp["w_up"], lp["w_ffn_dw"], lp["w_down"],
                  lp["w_final_norm"], seq, tm, per_seq_mod)
    return y.reshape(n_seq, seq, D_MODEL), outs[5:]


def kernel(x_prompt, x_sample, cache_k, cache_v, c, c_ctx, w_ada, b_ada, w_norm1, w_in, lambda_q1, lambda_k1, lambda_q2, lambda_k2, w_head_norm, w_attn_proj, w_conv_dw, conv_ln_g, conv_ln_b, w_conv_proj, w_out, w_norm2, w_up, w_ffn_dw, w_down, w_final_norm):
    assert w_in.shape[0] == 1, "single trunk layer"
    n_dec = x_sample.shape[0]
    n_ctx, seq_ctx = x_prompt.shape[0], x_prompt.shape[1]
    past = cache_k.shape[2]

    mod_rows = 16
    cc = jnp.concatenate(
        [c, c_ctx[None, :], jnp.zeros((mod_rows - n_dec - 1, D_MODEL), F32)], axis=0)
    mod = _mod_call(cc, w_ada[0], b_ada)
    mod_lat = mod[:n_dec].reshape(n_dec, 1, N_MOD * D_MODEL)
    mod_ctx = mod[n_dec:n_dec + 1].reshape(1, 1, N_MOD * D_MODEL)

    lp = dict(
        w_norm1=w_norm1, w_in=w_in[0].astype(BF16),
        lam=(lambda_q1, lambda_k1, lambda_q2, lambda_k2), w_head_norm=w_head_norm,
        w_attn_proj=w_attn_proj[0].astype(BF16), w_conv_dw=w_conv_dw[0],
        conv_ln_g=conv_ln_g, conv_ln_b=conv_ln_b,
        w_conv_proj=w_conv_proj[0].astype(BF16), w_out=w_out[0].astype(BF16),
        w_norm2=w_norm2, w_up=w_up[0].astype(BF16), w_ffn_dw=w_ffn_dw[0],
        w_down=w_down[0].astype(BF16), w_final_norm=w_final_norm[None, :])

    y_prompt, (kf, vf) = _trunk_group(
        x_prompt, mod_ctx, False, False, None, lp, tm=seq_ctx, tq=seq_ctx, tk=seq_ctx)
    cache = (cache_k[:, 0].reshape(n_dec, past, ATTN_W),
             cache_v[:, 0].reshape(n_dec, past, ATTN_W))
    y_sample, _ = _trunk_group(
        x_sample, mod_lat, True, True, cache, lp, tm=512, tq=256, tk=512)
    new_k = kf.reshape(n_ctx, 1, seq_ctx, N_HEADS, HEAD_W)
    new_v = vf.reshape(n_ctx, 1, seq_ctx, N_HEADS, V_DIM)
    return (y_prompt, y_sample, new_k, new_v)
```

```python
import functools

import jax
import jax.numpy as jnp
import numpy as np
from jax import lax
from jax.experimental import pallas as pl
from jax.experimental.pallas import tpu as pltpu

D_MODEL = 1024
N_HEADS = 4
QK_DIM = 64
V_DIM = 2 * QK_DIM
HEAD_W = 2 * QK_DIM
ATTN_W = N_HEADS * V_DIM
CONV_CH = D_MODEL // 2
DW_WIDTH = 31
D_FF = 2816
FFN_DW_WIDTH = 3
GRID_W = 64
ROPE_THETA = 10000.0
EPS = 1e-6
N_MOD = 6
LAM_INIT = 0.8 - 0.6 * float(np.exp(-0.3 * 0))
QK_SCALE = QK_DIM ** -0.5 * float(np.log2(np.e))

LANES = 128
HALO = 16
CONV_PAD = (DW_WIDTH - 1) // 2
VMEM_LIMIT = 56 * 1024 * 1024

F32 = jnp.float32
BF16 = jnp.bfloat16


def _sigmoid(x):
    return 1.0 / (1.0 + jnp.exp(-x))


def _const_spec(shape):
    nd = len(shape)
    return pl.BlockSpec(shape, lambda *_: (0,) * nd, pipeline_mode=pl.Buffered(1))


def _params(n_axes):
    return pltpu.CompilerParams(
        dimension_semantics=("arbitrary",) * n_axes, vmem_limit_bytes=VMEM_LIMIT)


def _mod_kernel(c_ref, w_ref, b_ref, o_ref):
    c = c_ref[...]
    s = c * _sigmoid(c)
    w = w_ref[...]
    s_hi = s.astype(BF16)
    s_lo = (s - s_hi.astype(F32)).astype(BF16)
    w_hi = w.astype(BF16)
    w_lo = (w - w_hi.astype(F32)).astype(BF16)
    acc = jnp.dot(s_hi, w_hi, preferred_element_type=F32)
    acc += jnp.dot(s_hi, w_lo, preferred_element_type=F32)
    acc += jnp.dot(s_lo, w_hi, preferred_element_type=F32)
    o_ref[...] = acc + b_ref[...]


def _mod_call(cc, w_ada, b_ada):
    rows = cc.shape[0]
    n_out = w_ada.shape[1]
    bn = 1536
    return pl.pallas_call(
        _mod_kernel,
        grid=(n_out // bn,),
        in_specs=[
            pl.BlockSpec((rows, D_MODEL), lambda j: (0, 0)),
            pl.BlockSpec((D_MODEL, bn), lambda j: (0, j)),
            pl.BlockSpec((1, bn), lambda j: (0, j)),
        ],
        out_specs=pl.BlockSpec((rows, bn), lambda j: (0, j)),
        out_shape=jax.ShapeDtypeStruct((rows, n_out), F32),
        compiler_params=_params(1),
        name="mod",
    )(cc, w_ada, b_ada)


def _rope_tables(seq):
    t = np.arange(seq)
    row = (t // GRID_W).astype(np.float32).astype(np.float64)
    col = (t % GRID_W).astype(np.float32).astype(np.float64)
    half = QK_DIM // 2
    freqs = ROPE_THETA ** (-np.arange(0, half, 2, dtype=np.float64) / half)
    ar = row[:, None] * freqs
    ac = col[:, None] * freqs
    cos = np.concatenate([np.cos(ar), np.cos(ar), np.cos(ac), np.cos(ac)], axis=1)
    sin = np.concatenate([-np.sin(ar), np.sin(ar), -np.sin(ac), np.sin(ac)], axis=1)
    cos = np.tile(cos, (1, HEAD_W // QK_DIM)).astype(np.float32)
    sin = np.tile(sin, (1, HEAD_W // QK_DIM)).astype(np.float32)
    return jnp.asarray(cos), jnp.asarray(sin)


def _rope(x, cos, sin):
    quarter = QK_DIM // 4
    lane = lax.broadcasted_iota(jnp.int32, (1, HEAD_W), 1)
    fwd = pltpu.roll(x, HEAD_W - quarter, 1)
    bwd = pltpu.roll(x, quarter, 1)
    partner = jnp.where((lane & quarter) == 0, fwd, bwd)
    return x * cos + partner * sin


def _in_proj_kernel(*refs, use_rope, emit_f32_kv):
    it = iter(refs)
    x_ref, mod_ref, wn_ref, w_ref = next(it), next(it), next(it), next(it)
    cos_ref = sin_ref = None
    if use_rope:
        cos_ref, sin_ref = next(it), next(it)
    qt_ref, k_ref, vt_ref, glu_ref, gate_ref = next(it), next(it), next(it), next(it), next(it)
    kf_ref = vf_ref = None
    if emit_f32_kv:
        kf_ref, vf_ref = next(it), next(it)

    x = x_ref[...]
    shift = mod_ref[:, 0:D_MODEL]
    scale = mod_ref[:, D_MODEL:2 * D_MODEL]
    ms = jnp.mean(x * x, axis=-1, keepdims=True)
    h = x * lax.rsqrt(ms + EPS) * wn_ref[...]
    h = h * (1.0 + scale) + shift
    hb = h.astype(BF16)

    def proj(c0, c1):
        return jnp.dot(hb, w_ref[:, c0:c1], preferred_element_type=F32)

    pq = proj(0, ATTN_W)
    pk = proj(ATTN_W, 2 * ATTN_W)
    if emit_f32_kv:
        kf_ref[...] = pk
    for hd in range(N_HEADS):
        sl = slice(hd * HEAD_W, (hd + 1) * HEAD_W)
        qh = pq[:, sl]
        kh = pk[:, sl]
        if use_rope:
            cos = cos_ref[...]
            sin = sin_ref[...]
            qh = _rope(qh, cos, sin)
            kh = _rope(kh, cos, sin)
        qt_ref[sl, :] = (qh * QK_SCALE).T.astype(BF16)
        k_ref[:, sl] = kh.astype(BF16)

    pv = proj(2 * ATTN_W, 3 * ATTN_W)
    if emit_f32_kv:
        vf_ref[...] = pv
    for hd in range(N_HEADS):
        sl = slice(hd * HEAD_W, (hd + 1) * HEAD_W)
        vt_ref[sl, :] = pv[:, sl].T.astype(BF16)

    u0 = 3 * ATTN_W
    pu = proj(u0, u0 + 2 * CONV_CH)
    glu_ref[...] = (pu[:, :CONV_CH] * _sigmoid(pu[:, CONV_CH:])).astype(BF16)

    g0 = u0 + 2 * CONV_CH
    for j in range(2):
        pg = proj(g0 + j * D_MODEL, g0 + (j + 1) * D_MODEL)
        gate_ref[:, j * D_MODEL:(j + 1) * D_MODEL] = _sigmoid(pg).astype(BF16)


def _in_proj_call(x2d, mod3, w_norm1, w_in_bf, seq, tm, per_seq_mod, use_rope, emit_f32_kv):
    n_tok = x2d.shape[0]
    tps = seq // tm
    in_cols = w_in_bf.shape[1]
    mod_idx = (lambda i: (i // tps, 0, 0)) if per_seq_mod else (lambda i: (0, 0, 0))
    in_specs = [
        pl.BlockSpec((tm, D_MODEL), lambda i: (i, 0)),
        pl.BlockSpec((None, 1, N_MOD * D_MODEL), mod_idx),
        _const_spec((1, D_MODEL)),
        _const_spec((D_MODEL, in_cols)),
    ]
    args = [x2d, mod3, w_norm1, w_in_bf]
    if use_rope:
        cos, sin = _rope_tables(seq)
        in_specs += [pl.BlockSpec((tm, HEAD_W), lambda i: (i % tps, 0))] * 2
        args += [cos, sin]
    tok_spec = lambda w: pl.BlockSpec((tm, w), lambda i: (i, 0))
    tr_spec = pl.BlockSpec((ATTN_W, tm), lambda i: (0, i))
    tr_shape = jax.ShapeDtypeStruct((ATTN_W, n_tok), BF16)
    out_specs = [tr_spec, tok_spec(ATTN_W), tr_spec, tok_spec(CONV_CH), tok_spec(2 * D_MODEL)]
    out_shape = [tr_shape, jax.ShapeDtypeStruct((n_tok, ATTN_W), BF16), tr_shape,
                 jax.ShapeDtypeStruct((n_tok, CONV_CH), BF16),
                 jax.ShapeDtypeStruct((n_tok, 2 * D_MODEL), BF16)]
    if emit_f32_kv:
        out_specs += [tok_spec(ATTN_W)] * 2
        out_shape += [jax.ShapeDtypeStruct((n_tok, ATTN_W), F32)] * 2
    return pl.pallas_call(
        functools.partial(_in_proj_kernel, use_rope=use_rope, emit_f32_kv=emit_f32_kv),
        grid=(n_tok // tm,),
        in_specs=in_specs,
        out_specs=out_specs,
        out_shape=out_shape,
        compiler_params=_params(1),
        name="in_proj",
    )(*args)


def _attn_kernel(*refs, tq, tk, n_chunks, use_cache):
    it = iter(refs)
    qt_ref, k_ref, vt_ref = next(it), next(it), next(it)
    ck_ref = cv_ref = None
    if use_cache:
        ck_ref, cv_ref = next(it), next(it)
    lq1, lk1, lq2, lk2, whn_ref = next(it), next(it), next(it), next(it), next(it)
    o_ref = next(it)

    qt = qt_ref[...]
    zero = jnp.zeros((QK_DIM, tq), BF16)
    qs = jnp.concatenate(
        [jnp.concatenate([qt[0:QK_DIM, :], zero], axis=0),
         jnp.concatenate([zero, qt[QK_DIM:HEAD_W, :]], axis=0)], axis=1)

    def update(state, kb, vtb):
        m_prev, l_prev, acc = state
        s = jnp.dot(kb, qs, preferred_element_type=F32)
        m_new = jnp.maximum(m_prev, jnp.max(s, axis=0, keepdims=True))
        alpha = jnp.exp2(m_prev - m_new)
        p = jnp.exp2(s - m_new)
        l_new = alpha * l_prev + jnp.sum(p, axis=0, keepdims=True)
        acc = alpha * acc + jnp.dot(vtb, p.astype(BF16), preferred_element_type=F32)
        return m_new, l_new, acc

    state = (jnp.full((1, 2 * tq), -jnp.inf, F32), jnp.zeros((1, 2 * tq), F32),
             jnp.zeros((V_DIM, 2 * tq), F32))
    for j in range(n_chunks):
        state = update(state, k_ref[j * tk:(j + 1) * tk, :], vt_ref[:, j * tk:(j + 1) * tk])
    if use_cache:
        state = update(state, ck_ref[0].astype(BF16), cv_ref[0].T.astype(BF16))
    _, l, acc = state

    ot = acc / l
    lam = (jnp.exp(jnp.sum(lq1[...] * lk1[...], axis=-1, keepdims=True))
           - jnp.exp(jnp.sum(lq2[...] * lk2[...], axis=-1, keepdims=True)) + LAM_INIT)
    o = (ot[:, 0:tq] - lam * ot[:, tq:2 * tq]).T
    ms = jnp.mean(o * o, axis=-1, keepdims=True)
    o = o * lax.rsqrt(ms + EPS) * whn_ref[...] * (1.0 - LAM_INIT)
    o_ref[...] = o.astype(BF16)


def _attn_call(qt, k, vt, cache_k, cache_v, lam_params, w_head_norm, n_seq, seq, tq, tk):
    n_tok = k.shape[0]
    qps = seq // tq
    use_cache = cache_k is not None
    in_specs = [
        pl.BlockSpec((HEAD_W, tq), lambda b, h, i: (h, b * qps + i)),
        pl.BlockSpec((seq, HEAD_W), lambda b, h, i: (b, h)),
        pl.BlockSpec((HEAD_W, seq), lambda b, h, i: (h, b)),
    ]
    args = [qt, k, vt]
    if use_cache:
        past = cache_k.shape[1]
        in_specs += [pl.BlockSpec((1, past, HEAD_W), lambda b, h, i: (b, 0, h))] * 2
        args += [cache_k, cache_v]
    in_specs += [_const_spec((1, QK_DIM))] * 4 + [_const_spec((1, V_DIM))]
    args += list(lam_params) + [w_head_norm]
    return pl.pallas_call(
        functools.partial(_attn_kernel, tq=tq, tk=tk, n_chunks=seq // tk, use_cache=use_cache),
        grid=(n_seq, N_HEADS, qps),
        in_specs=in_specs,
        out_specs=pl.BlockSpec((tq, HEAD_W), lambda b, h, i: (b * qps + i, h)),
        out_shape=jax.ShapeDtypeStruct((n_tok, ATTN_W), BF16),
        compiler_params=_params(3),
        name="attn",
    )(*args)


def _halo_specs(tm, width, n_tok):
    r = tm // HALO
    last = n_tok // HALO - 1
    return [
        pl.BlockSpec((tm, width), lambda i: (i, 0)),
        pl.BlockSpec((HALO, width), lambda i: (jnp.maximum(i * r - 1, 0), 0)),
        pl.BlockSpec((HALO, width), lambda i: (jnp.minimum((i + 1) * r, last), 0)),
    ]


def _fill_ext(ext_ref, main_ref, prev_ref, next_ref, tm, tps):
    j = pl.program_id(0) % tps
    prev = prev_ref[...].astype(ext_ref.dtype)
    nxt = next_ref[...].astype(ext_ref.dtype)
    ext_ref[0:HALO, :] = jnp.where(j > 0, prev, jnp.zeros_like(prev))
    ext_ref[HALO:HALO + tm, :] = main_ref[...].astype(ext_ref.dtype)
    ext_ref[HALO + tm:, :] = jnp.where(j < tps - 1, nxt, jnp.zeros_like(nxt))


def _fill_ext_slabs(ext_ref, main_ref, prev_ref, next_ref, tm, tps):
    j = pl.program_id(0) % tps
    for s in range(ext_ref.shape[0]):
        cols = slice(s * LANES, (s + 1) * LANES)
        prev = prev_ref[:, cols].astype(ext_ref.dtype)
        nxt = next_ref[:, cols].astype(ext_ref.dtype)
        ext_ref[s, 0:HALO, :] = jnp.where(j > 0, prev, jnp.zeros_like(prev))
        ext_ref[s, HALO:HALO + tm, :] = main_ref[:, cols].astype(ext_ref.dtype)
        ext_ref[s, HALO + tm:, :] = jnp.where(j < tps - 1, nxt, jnp.zeros_like(nxt))


def _merge_kernel(o_ref, glu_ref, glu_prev, glu_next, gate_ref, x_ref, mod_ref,
                  wap_ref, wdw_ref, lng_ref, lnb_ref, wcp_ref, wout_ref, wn2_ref,
                  x1_ref, h2_ref, ext_ref, conv_ref, *, tm, tps, rows):
    _fill_ext_slabs(ext_ref, glu_ref, glu_prev, glu_next, tm, tps)
    base = HALO - CONV_PAD
    for s in range(CONV_CH // LANES):
        cols = slice(s * LANES, (s + 1) * LANES)
        for r0 in range(0, tm, rows):
            acc = None
            for j in range(DW_WIDTH):
                tap = ext_ref[s, pl.ds(base + r0 + j, rows, stride=1), :] * wdw_ref[j:j + 1, cols]
                acc = tap if acc is None else acc + tap
            conv_ref[r0:r0 + rows, cols] = acc
    acc = conv_ref[...]
    mu = jnp.mean(acc, axis=-1, keepdims=True)
    d = acc - mu
    var = jnp.mean(d * d, axis=-1, keepdims=True)
    y = d * lax.rsqrt(var + EPS) * lng_ref[...] + lnb_ref[...]
    cv = (y * _sigmoid(y)).astype(BF16)
    conv_out = jnp.dot(cv, wcp_ref[...], preferred_element_type=F32)
    attn_out = jnp.dot(o_ref[...], wap_ref[...], preferred_element_type=F32)
    merged = (gate_ref[:, 0:D_MODEL].astype(F32) * attn_out
              + gate_ref[:, D_MODEL:2 * D_MODEL].astype(F32) * conv_out)
    mix = jnp.dot(merged.astype(BF16), wout_ref[...], preferred_element_type=F32)
    gate1 = mod_ref[:, 2 * D_MODEL:3 * D_MODEL]
    shift2 = mod_ref[:, 3 * D_MODEL:4 * D_MODEL]
    scale2 = mod_ref[:, 4 * D_MODEL:5 * D_MODEL]
    x1 = x_ref[...] + gate1 * mix
    x1_ref[...] = x1
    ms = jnp.mean(x1 * x1, axis=-1, keepdims=True)
    h2 = x1 * lax.rsqrt(ms + EPS) * wn2_ref[...]
    h2_ref[...] = (h2 * (1.0 + scale2) + shift2).astype(BF16)


def _merge_call(o_n, glu, gates, x2d, mod3, w_attn_proj, w_conv_dw, ln_g, ln_b,
                w_conv_proj, w_out, w_norm2, seq, tm, per_seq_mod):
    n_tok = x2d.shape[0]
    tps = seq // tm
    mod_idx = (lambda i: (i // tps, 0, 0)) if per_seq_mod else (lambda i: (0, 0, 0))
    tok_spec = lambda w: pl.BlockSpec((tm, w), lambda i: (i, 0))
    in_specs = (
        [tok_spec(ATTN_W)] + _halo_specs(tm, CONV_CH, n_tok)
        + [tok_spec(2 * D_MODEL), tok_spec(D_MODEL),
           pl.BlockSpec((None, 1, N_MOD * D_MODEL), mod_idx),
           _const_spec((ATTN_W, D_MODEL)), _const_spec((DW_WIDTH, CONV_CH)),
           _const_spec((1, CONV_CH)), _const_spec((1, CONV_CH)),
           _const_spec((CONV_CH, D_MODEL)), _const_spec((D_MODEL, D_MODEL)),
           _const_spec((1, D_MODEL))])
    return pl.pallas_call(
        functools.partial(_merge_kernel, tm=tm, tps=tps, rows=128),
        grid=(n_tok // tm,),
        in_specs=in_specs,
        out_specs=[tok_spec(D_MODEL), tok_spec(D_MODEL)],
        out_shape=[jax.ShapeDtypeStruct((n_tok, D_MODEL), F32),
                   jax.ShapeDtypeStruct((n_tok, D_MODEL), BF16)],
        scratch_shapes=[pltpu.VMEM((CONV_CH // LANES, tm + 2 * HALO, LANES), F32),
                        pltpu.VMEM((tm, CONV_CH), F32)],
        compiler_params=_params(1),
        name="merge",
    )(o_n, glu, glu, glu, gates, x2d, mod3, w_attn_proj, w_conv_dw, ln_g, ln_b,
      w_conv_proj, w_out, w_norm2)


def _ffn_kernel(h_ref, h_prev, h_next, x1_ref, mod_ref, wup_ref, wdw_ref, wdn_ref, wfn_ref,
                y_ref, lhs_ref, u_ref, acc_ref, *, tm, tps, cn):
    _fill_ext(lhs_ref, h_ref, h_prev, h_next, tm, tps)
    lhs = lhs_ref[...]
    n_slabs = cn // LANES

    def up_proj(par, half, c0):
        res = jnp.dot(lhs, wup_ref[:, c0:c0 + cn], preferred_element_type=F32)
        for s in range(n_slabs):
            u_ref[par, half, s] = res[:, s * LANES:(s + 1) * LANES]

    def conv(par, half, c0):
        outs = []
        for s in range(n_slabs):
            cols = slice(c0 + s * LANES, c0 + (s + 1) * LANES)
            taps = [u_ref[par, half, s, pl.ds(HALO - 1 + j, tm, stride=1), :] * wdw_ref[j:j + 1, cols]
                    for j in range(FFN_DW_WIDTH)]
            outs.append(taps[0] + taps[1] + taps[2])
        return jnp.concatenate(outs, axis=1)

    for c in range(D_FF // cn):
        a0 = c * cn
        b0 = D_FF + c * cn
        par = c % 2
        up_proj(par, 0, a0)
        up_proj(par, 1, b0)
        a = conv(par, 0, a0)
        b = conv(par, 1, b0)
        act = (a * _sigmoid(a) * b).astype(BF16)
        down = jnp.dot(act, wdn_ref[a0:a0 + cn, :], preferred_element_type=F32)
        if c == 0:
            acc_ref[...] = down
        else:
            acc_ref[...] += down

    gate2 = mod_ref[:, 5 * D_MODEL:6 * D_MODEL]
    y = x1_ref[...] + gate2 * acc_ref[...]
    ms = jnp.mean(y * y, axis=-1, keepdims=True)
    y_ref[...] = y * lax.rsqrt(ms + EPS) * wfn_ref[...]


def _ffn_call(h2, x1, mod3, w_up, w_ffn_dw, w_down, w_final_norm, seq, tm, per_seq_mod):
    n_tok = x1.shape[0]
    tps = seq // tm
    cn = 256
    mod_idx = (lambda i: (i // tps, 0, 0)) if per_seq_mod else (lambda i: (0, 0, 0))
    tok_spec = lambda w: pl.BlockSpec((tm, w), lambda i: (i, 0))
    in_specs = (
        _halo_specs(tm, D_MODEL, n_tok)
        + [tok_spec(D_MODEL), pl.BlockSpec((None, 1, N_MOD * D_MODEL), mod_idx),
           _const_spec((D_MODEL, 2 * D_FF)), _const_spec((FFN_DW_WIDTH, 2 * D_FF)),
           _const_spec((D_FF, D_MODEL)), _const_spec((1, D_MODEL))])
    return pl.pallas_call(
        functools.partial(_ffn_kernel, tm=tm, tps=tps, cn=cn),
        grid=(n_tok // tm,),
        in_specs=in_specs,
        out_specs=tok_spec(D_MODEL),
        out_shape=jax.ShapeDtypeStruct((n_tok, D_MODEL), F32),
        scratch_shapes=[
            pltpu.VMEM((tm + 2 * HALO, D_MODEL), BF16),
            pltpu.VMEM((2, 2, cn // LANES, tm + 2 * HALO, LANES), F32),
            pltpu.VMEM((tm, D_MODEL), F32),
        ],
        compiler_params=_params(1),
        name="ffn",
    )(h2, h2, h2, x1, mod3, w_up, w_ffn_dw, w_down, w_final_norm)


def _trunk_group(x, mod3, per_seq_mod, use_rope, cache, lp, tm, tq, tk):
    n_seq, seq, _ = x.shape
    x2d = x.reshape(n_seq * seq, D_MODEL)
    emit_f32_kv = cache is None
    outs = _in_proj_call(x2d, mod3, lp["w_norm1"], lp["w_in"], seq, tm,
                         per_seq_mod, use_rope, emit_f32_kv)
    q, k, v, glu, gates = outs[:5]
    cache_k = cache_v = None
    if cache is not None:
        cache_k, cache_v = cache
    o_n = _attn_call(q, k, v, cache_k, cache_v, lp["lam"], lp["w_head_norm"],
                     n_seq, seq, tq, tk)
    x1, h2 = _merge_call(o_n, glu, gates, x2d, mod3, lp["w_attn_proj"], lp["w_conv_dw"],
                         lp["conv_ln_g"], lp["conv_ln_b"], lp["w_conv_proj"], lp["w_out"],
                         lp["w_norm2"], seq, tm, per_seq_mod)
    y = _ffn_call(h2, x1, mod3, lp["w_up"], lp["w_ffn_dw"], lp["w_down"],
                  lp["w_final_norm"], seq, tm, per_seq_mod)
    return y.reshape(n_seq, seq, D_MODEL), outs[5:]


def kernel(x_prompt, x_sample, cache_k, cache_v, c, c_ctx, w_ada, b_ada, w_norm1, w_in, lambda_q1, lambda_k1, lambda_q2, lambda_k2, w_head_norm, w_attn_proj, w_conv_dw, conv_ln_g, conv_ln_b, w_conv_proj, w_out, w_norm2, w_up, w_ffn_dw, w_down, w_final_norm):
    assert w_in.shape[0] == 1, "single trunk layer"
    n_dec = x_sample.shape[0]
    n_ctx, seq_ctx = x_prompt.shape[0], x_prompt.shape[1]
    past = cache_k.shape[2]

    mod_rows = 16
    cc = jnp.concatenate(
        [c, c_ctx[None, :], jnp.zeros((mod_rows - n_dec - 1, D_MODEL), F32)], axis=0)
    mod = _mod_call(cc, w_ada[0], b_ada)
    mod_lat = mod[:n_dec].reshape(n_dec, 1, N_MOD * D_MODEL)
    mod_ctx = mod[n_dec:n_dec + 1].reshape(1, 1, N_MOD * D_MODEL)

    lp = dict(
        w_norm1=w_norm1, w_in=w_in[0].astype(BF16),
        lam=(lambda_q1, lambda_k1, lambda_q2, lambda_k2), w_head_norm=w_head_norm,
        w_attn_proj=w_attn_proj[0].astype(BF16), w_conv_dw=w_conv_dw[0],
        conv_ln_g=conv_ln_g, conv_ln_b=conv_ln_b,
        w_conv_proj=w_conv_proj[0].astype(BF16), w_out=w_out[0].astype(BF16),
        w_norm2=w_norm2, w_up=w_up[0].astype(BF16), w_ffn_dw=w_ffn_dw[0],
        w_down=w_down[0].astype(BF16), w_final_norm=w_final_norm[None, :])

    y_prompt, (kf, vf) = _trunk_group(
        x_prompt, mod_ctx, False, False, None, lp, tm=seq_ctx, tq=seq_ctx, tk=seq_ctx)
    cache = (cache_k[:, 0].reshape(n_dec, past, ATTN_W),
             cache_v[:, 0].reshape(n_dec, past, ATTN_W))
    y_sample, _ = _trunk_group(
        x_sample, mod_lat, True, True, cache, lp, tm=512, tq=512, tk=2048)
    new_k = kf.reshape(n_ctx, 1, seq_ctx, N_HEADS, HEAD_W)
    new_v = vf.reshape(n_ctx, 1, seq_ctx, N_HEADS, V_DIM)
    return (y_prompt, y_sample, new_k, new_v)
```

```python
import functools

import jax
import jax.numpy as jnp
import numpy as np
from jax import lax
from jax.experimental import pallas as pl
from jax.experimental.pallas import tpu as pltpu

D_MODEL = 1024
N_HEADS = 4
QK_DIM = 64
V_DIM = 2 * QK_DIM
HEAD_W = 2 * QK_DIM
ATTN_W = N_HEADS * V_DIM
CONV_CH = D_MODEL // 2
DW_WIDTH = 31
D_FF = 2816
FFN_DW_WIDTH = 3
GRID_W = 64
ROPE_THETA = 10000.0
EPS = 1e-6
N_MOD = 6
LAM_INIT = 0.8 - 0.6 * float(np.exp(-0.3 * 0))
QK_SCALE = QK_DIM ** -0.5 * float(np.log2(np.e))

MAX_EXCESS = 16.0
LANES = 128
HALO = 16
CONV_PAD = (DW_WIDTH - 1) // 2
VMEM_LIMIT = 56 * 1024 * 1024

F32 = jnp.float32
BF16 = jnp.bfloat16


def _sigmoid(x):
    return 1.0 / (1.0 + jnp.exp(-x))


def _const_spec(shape):
    nd = len(shape)
    return pl.BlockSpec(shape, lambda *_: (0,) * nd, pipeline_mode=pl.Buffered(1))


def _params(n_axes):
    return pltpu.CompilerParams(
        dimension_semantics=("arbitrary",) * n_axes, vmem_limit_bytes=VMEM_LIMIT)


def _mod_kernel(c_ref, w_ref, b_ref, o_ref):
    c = c_ref[...]
    s = c * _sigmoid(c)
    w = w_ref[...]
    s_hi = s.astype(BF16)
    s_lo = (s - s_hi.astype(F32)).astype(BF16)
    w_hi = w.astype(BF16)
    w_lo = (w - w_hi.astype(F32)).astype(BF16)
    acc = jnp.dot(s_hi, w_hi, preferred_element_type=F32)
    acc += jnp.dot(s_hi, w_lo, preferred_element_type=F32)
    acc += jnp.dot(s_lo, w_hi, preferred_element_type=F32)
    o_ref[...] = acc + b_ref[...]


def _mod_call(cc, w_ada, b_ada):
    rows = cc.shape[0]
    n_out = w_ada.shape[1]
    bn = 1536
    return pl.pallas_call(
        _mod_kernel,
        grid=(n_out // bn,),
        in_specs=[
            pl.BlockSpec((rows, D_MODEL), lambda j: (0, 0)),
            pl.BlockSpec((D_MODEL, bn), lambda j: (0, j)),
            pl.BlockSpec((1, bn), lambda j: (0, j)),
        ],
        out_specs=pl.BlockSpec((rows, bn), lambda j: (0, j)),
        out_shape=jax.ShapeDtypeStruct((rows, n_out), F32),
        compiler_params=_params(1),
        name="mod",
    )(cc, w_ada, b_ada)


def _rope_tables(seq):
    t = np.arange(seq)
    row = (t // GRID_W).astype(np.float32).astype(np.float64)
    col = (t % GRID_W).astype(np.float32).astype(np.float64)
    half = QK_DIM // 2
    freqs = ROPE_THETA ** (-np.arange(0, half, 2, dtype=np.float64) / half)
    ar = row[:, None] * freqs
    ac = col[:, None] * freqs
    cos = np.concatenate([np.cos(ar), np.cos(ar), np.cos(ac), np.cos(ac)], axis=1)
    sin = np.concatenate([-np.sin(ar), np.sin(ar), -np.sin(ac), np.sin(ac)], axis=1)
    cos = np.tile(cos, (1, HEAD_W // QK_DIM)).astype(np.float32)
    sin = np.tile(sin, (1, HEAD_W // QK_DIM)).astype(np.float32)
    return jnp.asarray(cos), jnp.asarray(sin)


def _rope(x, cos, sin):
    quarter = QK_DIM // 4
    lane = lax.broadcasted_iota(jnp.int32, (1, HEAD_W), 1)
    fwd = pltpu.roll(x, HEAD_W - quarter, 1)
    bwd = pltpu.roll(x, quarter, 1)
    partner = jnp.where((lane & quarter) == 0, fwd, bwd)
    return x * cos + partner * sin


def _in_proj_kernel(*refs, use_rope, emit_f32_kv):
    it = iter(refs)
    x_ref, mod_ref, wn_ref, w_ref = next(it), next(it), next(it), next(it)
    cos_ref = sin_ref = None
    if use_rope:
        cos_ref, sin_ref = next(it), next(it)
    qt_ref, k_ref, vt_ref, glu_ref, gate_ref = next(it), next(it), next(it), next(it), next(it)
    kf_ref = vf_ref = None
    if emit_f32_kv:
        kf_ref, vf_ref = next(it), next(it)

    x = x_ref[...]
    shift = mod_ref[:, 0:D_MODEL]
    scale = mod_ref[:, D_MODEL:2 * D_MODEL]
    ms = jnp.mean(x * x, axis=-1, keepdims=True)
    h = x * lax.rsqrt(ms + EPS) * wn_ref[...]
    h = h * (1.0 + scale) + shift
    hb = h.astype(BF16)

    def proj(c0, c1):
        return jnp.dot(hb, w_ref[:, c0:c1], preferred_element_type=F32)

    pq = proj(0, ATTN_W)
    pk = proj(ATTN_W, 2 * ATTN_W)
    if emit_f32_kv:
        kf_ref[...] = pk
    for hd in range(N_HEADS):
        sl = slice(hd * HEAD_W, (hd + 1) * HEAD_W)
        qh = pq[:, sl]
        kh = pk[:, sl]
        if use_rope:
            cos = cos_ref[...]
            sin = sin_ref[...]
            qh = _rope(qh, cos, sin)
            kh = _rope(kh, cos, sin)
        qt_ref[sl, :] = (qh * QK_SCALE).T.astype(BF16)
        k_ref[:, sl] = kh.astype(BF16)

    pv = proj(2 * ATTN_W, 3 * ATTN_W)
    if emit_f32_kv:
        vf_ref[...] = pv
    for hd in range(N_HEADS):
        sl = slice(hd * HEAD_W, (hd + 1) * HEAD_W)
        vt_ref[sl, :] = pv[:, sl].T.astype(BF16)

    u0 = 3 * ATTN_W
    pu = proj(u0, u0 + 2 * CONV_CH)
    glu_ref[...] = (pu[:, :CONV_CH] * _sigmoid(pu[:, CONV_CH:])).astype(BF16)

    g0 = u0 + 2 * CONV_CH
    for j in range(2):
        pg = proj(g0 + j * D_MODEL, g0 + (j + 1) * D_MODEL)
        gate_ref[:, j * D_MODEL:(j + 1) * D_MODEL] = _sigmoid(pg).astype(BF16)


def _in_proj_call(x2d, mod3, w_norm1, w_in_bf, seq, tm, per_seq_mod, use_rope, emit_f32_kv):
    n_tok = x2d.shape[0]
    tps = seq // tm
    in_cols = w_in_bf.shape[1]
    mod_idx = (lambda i: (i // tps, 0, 0)) if per_seq_mod else (lambda i: (0, 0, 0))
    in_specs = [
        pl.BlockSpec((tm, D_MODEL), lambda i: (i, 0)),
        pl.BlockSpec((None, 1, N_MOD * D_MODEL), mod_idx),
        _const_spec((1, D_MODEL)),
        _const_spec((D_MODEL, in_cols)),
    ]
    args = [x2d, mod3, w_norm1, w_in_bf]
    if use_rope:
        cos, sin = _rope_tables(seq)
        in_specs += [pl.BlockSpec((tm, HEAD_W), lambda i: (i % tps, 0))] * 2
        args += [cos, sin]
    tok_spec = lambda w: pl.BlockSpec((tm, w), lambda i: (i, 0))
    tr_spec = pl.BlockSpec((ATTN_W, tm), lambda i: (0, i))
    tr_shape = jax.ShapeDtypeStruct((ATTN_W, n_tok), BF16)
    out_specs = [tr_spec, tok_spec(ATTN_W), tr_spec, tok_spec(CONV_CH), tok_spec(2 * D_MODEL)]
    out_shape = [tr_shape, jax.ShapeDtypeStruct((n_tok, ATTN_W), BF16), tr_shape,
                 jax.ShapeDtypeStruct((n_tok, CONV_CH), BF16),
                 jax.ShapeDtypeStruct((n_tok, 2 * D_MODEL), BF16)]
    if emit_f32_kv:
        out_specs += [tok_spec(ATTN_W)] * 2
        out_shape += [jax.ShapeDtypeStruct((n_tok, ATTN_W), F32)] * 2
    return pl.pallas_call(
        functools.partial(_in_proj_kernel, use_rope=use_rope, emit_f32_kv=emit_f32_kv),
        grid=(n_tok // tm,),
        in_specs=in_specs,
        out_specs=out_specs,
        out_shape=out_shape,
        compiler_params=_params(1),
        name="in_proj",
    )(*args)


def _attn_kernel(*refs, tq, tk, n_chunks, use_cache):
    it = iter(refs)
    qt_ref, k_ref, vt_ref = next(it), next(it), next(it)
    ck_ref = cv_ref = None
    if use_cache:
        ck_ref, cv_ref = next(it), next(it)
    lq1, lk1, lq2, lk2, whn_ref = next(it), next(it), next(it), next(it), next(it)
    o_ref = next(it)

    qt = qt_ref[...]
    zero = jnp.zeros((QK_DIM, tq), BF16)
    qs = jnp.concatenate(
        [jnp.concatenate([qt[0:QK_DIM, :], zero], axis=0),
         jnp.concatenate([zero, qt[QK_DIM:HEAD_W, :]], axis=0)], axis=1)

    chunks = [(k_ref[j * tk:(j + 1) * tk, :], vt_ref[:, j * tk:(j + 1) * tk])
              for j in range(n_chunks)]
    if use_cache:
        chunks.insert(0, (ck_ref[0].astype(BF16), cv_ref[0].T.astype(BF16)))

    def finalize(l, acc):
        ot = acc / l
        lam = (jnp.exp(jnp.sum(lq1[...] * lk1[...], axis=-1, keepdims=True))
               - jnp.exp(jnp.sum(lq2[...] * lk2[...], axis=-1, keepdims=True)) + LAM_INIT)
        o = (ot[:, 0:tq] - lam * ot[:, tq:2 * tq]).T
        ms = jnp.mean(o * o, axis=-1, keepdims=True)
        o = o * lax.rsqrt(ms + EPS) * whn_ref[...] * (1.0 - LAM_INIT)
        o_ref[...] = o.astype(BF16)

    kb, vtb = chunks[0]
    s = jnp.dot(kb, qs, preferred_element_type=F32)
    ref_row = jnp.max(s, axis=0, keepdims=True).astype(BF16).astype(F32)
    p = jnp.exp2(s - ref_row)
    l = jnp.sum(p, axis=0, keepdims=True)
    acc = jnp.dot(vtb, p.astype(BF16), preferred_element_type=F32)
    if len(chunks) == 1:
        finalize(l, acc)
        return

    row = lax.broadcasted_iota(jnp.int32, (16, 2 * tq), 0)
    neg_ref = jnp.where(row == 0, -ref_row, 0.0).astype(BF16)
    qs_aug = jnp.concatenate(
        [qs, neg_ref, jnp.zeros((HEAD_W - 16, 2 * tq), BF16)], axis=0)
    excess = jnp.zeros((1, 2 * tq), F32)
    for kb, vtb in chunks[1:]:
        k_aug = jnp.concatenate([kb, jnp.ones((kb.shape[0], HEAD_W), BF16)], axis=1)
        s = jnp.dot(k_aug, qs_aug, preferred_element_type=F32)
        p = jnp.exp2(s)
        excess = jnp.maximum(excess, jnp.max(s, axis=0, keepdims=True))
        l = l + jnp.sum(p, axis=0, keepdims=True)
        acc = acc + jnp.dot(vtb, p.astype(BF16), preferred_element_type=F32)
    in_range = jnp.max(excess) <= MAX_EXCESS

    @pl.when(in_range)
    def _():
        finalize(l, acc)

    @pl.when(jnp.logical_not(in_range))
    def _():
        def update(state, kb, vtb):
            m_prev, l_prev, acc = state
            s = jnp.dot(kb, qs, preferred_element_type=F32)
            m_new = jnp.maximum(m_prev, jnp.max(s, axis=0, keepdims=True))
            alpha = jnp.exp2(m_prev - m_new)
            p = jnp.exp2(s - m_new)
            l_new = alpha * l_prev + jnp.sum(p, axis=0, keepdims=True)
            acc = alpha * acc + jnp.dot(vtb, p.astype(BF16), preferred_element_type=F32)
            return m_new, l_new, acc

        state = (jnp.full((1, 2 * tq), -jnp.inf, F32), jnp.zeros((1, 2 * tq), F32),
                 jnp.zeros((V_DIM, 2 * tq), F32))
        for kb, vtb in chunks:
            state = update(state, kb, vtb)
        finalize(state[1], state[2])


def _attn_call(qt, k, vt, cache_k, cache_v, lam_params, w_head_norm, n_seq, seq, tq, tk):
    n_tok = k.shape[0]
    qps = seq // tq
    use_cache = cache_k is not None
    in_specs = [
        pl.BlockSpec((HEAD_W, tq), lambda b, h, i: (h, b * qps + i)),
        pl.BlockSpec((seq, HEAD_W), lambda b, h, i: (b, h)),
        pl.BlockSpec((HEAD_W, seq), lambda b, h, i: (h, b)),
    ]
    args = [qt, k, vt]
    if use_cache:
        past = cache_k.shape[1]
        in_specs += [pl.BlockSpec((1, past, HEAD_W), lambda b, h, i: (b, 0, h))] * 2
        args += [cache_k, cache_v]
    in_specs += [_const_spec((1, QK_DIM))] * 4 + [_const_spec((1, V_DIM))]
    args += list(lam_params) + [w_head_norm]
    return pl.pallas_call(
        functools.partial(_attn_kernel, tq=tq, tk=tk, n_chunks=seq // tk, use_cache=use_cache),
        grid=(n_seq, N_HEADS, qps),
        in_specs=in_specs,
        out_specs=pl.BlockSpec((tq, HEAD_W), lambda b, h, i: (b * qps + i, h)),
        out_shape=jax.ShapeDtypeStruct((n_tok, ATTN_W), BF16),
        compiler_params=_params(3),
        name="attn",
    )(*args)


def _halo_specs(tm, width, n_tok):
    r = tm // HALO
    last = n_tok // HALO - 1
    return [
        pl.BlockSpec((tm, width), lambda i: (i, 0)),
        pl.BlockSpec((HALO, width), lambda i: (jnp.maximum(i * r - 1, 0), 0)),
        pl.BlockSpec((HALO, width), lambda i: (jnp.minimum((i + 1) * r, last), 0)),
    ]


def _fill_ext(ext_ref, main_ref, prev_ref, next_ref, tm, tps):
    j = pl.program_id(0) % tps
    prev = prev_ref[...].astype(ext_ref.dtype)
    nxt = next_ref[...].astype(ext_ref.dtype)
    ext_ref[0:HALO, :] = jnp.where(j > 0, prev, jnp.zeros_like(prev))
    ext_ref[HALO:HALO + tm, :] = main_ref[...].astype(ext_ref.dtype)
    ext_ref[HALO + tm:, :] = jnp.where(j < tps - 1, nxt, jnp.zeros_like(nxt))


def _fill_ext_slabs(ext_ref, main_ref, prev_ref, next_ref, tm, tps):
    j = pl.program_id(0) % tps
    for s in range(ext_ref.shape[0]):
        cols = slice(s * LANES, (s + 1) * LANES)
        prev = prev_ref[:, cols].astype(ext_ref.dtype)
        nxt = next_ref[:, cols].astype(ext_ref.dtype)
        ext_ref[s, 0:HALO, :] = jnp.where(j > 0, prev, jnp.zeros_like(prev))
        ext_ref[s, HALO:HALO + tm, :] = main_ref[:, cols].astype(ext_ref.dtype)
        ext_ref[s, HALO + tm:, :] = jnp.where(j < tps - 1, nxt, jnp.zeros_like(nxt))


def _merge_kernel(o_ref, glu_ref, glu_prev, glu_next, gate_ref, x_ref, mod_ref,
                  wap_ref, wdw_ref, lng_ref, lnb_ref, wcp_ref, wout_ref, wn2_ref,
                  x1_ref, h2_ref, ext_ref, conv_ref, attn_ref, *, tm, tps, rows, rb):
    _fill_ext_slabs(ext_ref, glu_ref, glu_prev, glu_next, tm, tps)
    base = HALO - CONV_PAD
    def conv(b0):
        for s in range(CONV_CH // LANES):
            cols = slice(s * LANES, (s + 1) * LANES)
            for r0 in range(b0, b0 + rb, rows):
                acc = None
                for j in range(DW_WIDTH):
                    tap = ext_ref[s, pl.ds(base + r0 + j, rows, stride=1), :] * wdw_ref[j:j + 1, cols]
                    acc = tap if acc is None else acc + tap
                conv_ref[r0:r0 + rows, cols] = acc

    gate1 = mod_ref[:, 2 * D_MODEL:3 * D_MODEL]
    shift2 = mod_ref[:, 3 * D_MODEL:4 * D_MODEL]
    scale2 = mod_ref[:, 4 * D_MODEL:5 * D_MODEL]

    def mix_rows(b0):
        rs = slice(b0, b0 + rb)
        acc = conv_ref[rs, :]
        mu = jnp.mean(acc, axis=-1, keepdims=True)
        d = acc - mu
        var = jnp.mean(d * d, axis=-1, keepdims=True)
        y = d * lax.rsqrt(var + EPS) * lng_ref[...] + lnb_ref[...]
        cv = (y * _sigmoid(y)).astype(BF16)
        conv_out = jnp.dot(cv, wcp_ref[...], preferred_element_type=F32)
        merged = attn_ref[rs, :] + gate_ref[rs, D_MODEL:2 * D_MODEL].astype(F32) * conv_out
        mix = jnp.dot(merged.astype(BF16), wout_ref[...], preferred_element_type=F32)
        x1 = x_ref[rs, :] + gate1 * mix
        x1_ref[rs, :] = x1
        ms = jnp.mean(x1 * x1, axis=-1, keepdims=True)
        h2 = x1 * lax.rsqrt(ms + EPS) * wn2_ref[...]
        h2_ref[rs, :] = (h2 * (1.0 + scale2) + shift2).astype(BF16)

    attn_ref[...] = gate_ref[:, 0:D_MODEL].astype(F32) * jnp.dot(
        o_ref[...], wap_ref[...], preferred_element_type=F32)
    conv(0)
    for b0 in range(0, tm, rb):
        if b0 + rb < tm:
            conv(b0 + rb)
        mix_rows(b0)


def _merge_call(o_n, glu, gates, x2d, mod3, w_attn_proj, w_conv_dw, ln_g, ln_b,
                w_conv_proj, w_out, w_norm2, seq, tm, per_seq_mod):
    n_tok = x2d.shape[0]
    tps = seq // tm
    mod_idx = (lambda i: (i // tps, 0, 0)) if per_seq_mod else (lambda i: (0, 0, 0))
    tok_spec = lambda w: pl.BlockSpec((tm, w), lambda i: (i, 0))
    in_specs = (
        [tok_spec(ATTN_W)] + _halo_specs(tm, CONV_CH, n_tok)
        + [tok_spec(2 * D_MODEL), tok_spec(D_MODEL),
           pl.BlockSpec((None, 1, N_MOD * D_MODEL), mod_idx),
           _const_spec((ATTN_W, D_MODEL)), _const_spec((DW_WIDTH, CONV_CH)),
           _const_spec((1, CONV_CH)), _const_spec((1, CONV_CH)),
           _const_spec((CONV_CH, D_MODEL)), _const_spec((D_MODEL, D_MODEL)),
           _const_spec((1, D_MODEL))])
    return pl.pallas_call(
        functools.partial(_merge_kernel, tm=tm, tps=tps, rows=128, rb=min(tm, 256)),
        grid=(n_tok // tm,),
        in_specs=in_specs,
        out_specs=[tok_spec(D_MODEL), tok_spec(D_MODEL)],
        out_shape=[jax.ShapeDtypeStruct((n_tok, D_MODEL), F32),
                   jax.ShapeDtypeStruct((n_tok, D_MODEL), BF16)],
        scratch_shapes=[pltpu.VMEM((CONV_CH // LANES, tm + 2 * HALO, LANES), F32),
                        pltpu.VMEM((tm, CONV_CH), F32),
                        pltpu.VMEM((tm, D_MODEL), F32)],
        compiler_params=_params(1),
        name="merge",
    )(o_n, glu, glu, glu, gates, x2d, mod3, w_attn_proj, w_conv_dw, ln_g, ln_b,
      w_conv_proj, w_out, w_norm2)


def _ffn_kernel(h_ref, h_prev, h_next, x1_ref, mod_ref, wup_ref, wdw_ref, wdn_ref, wfn_ref,
                y_ref, lhs_ref, u_ref, acc_ref, *, tm, tps, cn):
    _fill_ext(lhs_ref, h_ref, h_prev, h_next, tm, tps)
    lhs = lhs_ref[...]
    n_slabs = cn // LANES

    def up_proj(par, half, c0):
        res = jnp.dot(lhs, wup_ref[:, c0:c0 + cn], preferred_element_type=F32)
        for s in range(n_slabs):
            u_ref[par, half, s] = res[:, s * LANES:(s + 1) * LANES]

    def conv(par, half, c0):
        outs = []
        for s in range(n_slabs):
            cols = slice(c0 + s * LANES, c0 + (s + 1) * LANES)
            taps = [u_ref[par, half, s, pl.ds(HALO - 1 + j, tm, stride=1), :] * wdw_ref[j:j + 1, cols]
                    for j in range(FFN_DW_WIDTH)]
            outs.append(taps[0] + taps[1] + taps[2])
        return jnp.concatenate(outs, axis=1)

    n_chunks = D_FF // cn
    up_proj(0, 0, 0)
    up_proj(0, 1, D_FF)
    for c in range(n_chunks):
        a0 = c * cn
        b0 = D_FF + c * cn
        par = c % 2
        if c + 1 < n_chunks:
            up_proj(1 - par, 0, a0 + cn)
            up_proj(1 - par, 1, b0 + cn)
        a = conv(par, 0, a0)
        b = conv(par, 1, b0)
        act = (a * _sigmoid(a) * b).astype(BF16)
        down = jnp.dot(act, wdn_ref[a0:a0 + cn, :], preferred_element_type=F32)
        if c == 0:
            acc_ref[...] = down
        else:
            acc_ref[...] += down

    gate2 = mod_ref[:, 5 * D_MODEL:6 * D_MODEL]
    y = x1_ref[...] + gate2 * acc_ref[...]
    ms = jnp.mean(y * y, axis=-1, keepdims=True)
    y_ref[...] = y * lax.rsqrt(ms + EPS) * wfn_ref[...]


def _ffn_call(h2, x1, mod3, w_up, w_ffn_dw, w_down, w_final_norm, seq, tm, per_seq_mod):
    n_tok = x1.shape[0]
    tps = seq // tm
    cn = 256
    mod_idx = (lambda i: (i // tps, 0, 0)) if per_seq_mod else (lambda i: (0, 0, 0))
    tok_spec = lambda w: pl.BlockSpec((tm, w), lambda i: (i, 0))
    in_specs = (
        _halo_specs(tm, D_MODEL, n_tok)
        + [tok_spec(D_MODEL), pl.BlockSpec((None, 1, N_MOD * D_MODEL), mod_idx),
           _const_spec((D_MODEL, 2 * D_FF)), _const_spec((FFN_DW_WIDTH, 2 * D_FF)),
           _const_spec((D_FF, D_MODEL)), _const_spec((1, D_MODEL))])
    return pl.pallas_call(
        functools.partial(_ffn_kernel, tm=tm, tps=tps, cn=cn),
        grid=(n_tok // tm,),
        in_specs=in_specs,
        out_specs=tok_spec(D_MODEL),
        out_shape=jax.ShapeDtypeStruct((n_tok, D_MODEL), F32),
        scratch_shapes=[
            pltpu.VMEM((tm + 2 * HALO, D_MODEL), BF16),
            pltpu.VMEM((2, 2, cn // LANES, tm + 2 * HALO, LANES), F32),
            pltpu.VMEM((tm, D_MODEL), F32),
        ],
        compiler_params=_params(1),
        name="ffn",
    )(h2, h2, h2, x1, mod3, w_up, w_ffn_dw, w_down, w_final_norm)


def _trunk_group(x, mod3, per_seq_mod, use_rope, cache, lp, tm, tq, tk):
    n_seq, seq, _ = x.shape
    x2d = x.reshape(n_seq * seq, D_MODEL)
    emit_f32_kv = cache is None
    outs = _in_proj_call(x2d, mod3, lp["w_norm1"], lp["w_in"], seq, tm,
                         per_seq_mod, use_rope, emit_f32_kv)
    q, k, v, glu, gates = outs[:5]
    cache_k = cache_v = None
    if cache is not None:
        cache_k, cache_v = cache
    o_n = _attn_call(q, k, v, cache_k, cache_v, lp["lam"], lp["w_head_norm"],
                     n_seq, seq, tq, tk)
    x1, h2 = _merge_call(o_n, glu, gates, x2d, mod3, lp["w_attn_proj"], lp["w_conv_dw"],
                         lp["conv_ln_g"], lp["conv_ln_b"], lp["w_conv_proj"], lp["w_out"],
                         lp["w_norm2"], seq, tm, per_seq_mod)
    y = _ffn_call(h2, x1, mod3, lp["w_up"], lp["w_ffn_dw"], lp["w_down"],
                  lp["w_final_norm"], seq, tm, per_seq_mod)
    return y.reshape(n_seq, seq, D_MODEL), outs[5:]


def kernel(x_prompt, x_sample, cache_k, cache_v, c, c_ctx, w_ada, b_ada, w_norm1, w_in, lambda_q1, lambda_k1, lambda_q2, lambda_k2, w_head_norm, w_attn_proj, w_conv_dw, conv_ln_g, conv_ln_b, w_conv_proj, w_out, w_norm2, w_up, w_ffn_dw, w_down, w_final_norm):
    assert w_in.shape[0] == 1, "single trunk layer"
    n_dec = x_sample.shape[0]
    n_ctx, seq_ctx = x_prompt.shape[0], x_prompt.shape[1]
    past = cache_k.shape[2]

    mod_rows = 16
    cc = jnp.concatenate(
        [c, c_ctx[None, :], jnp.zeros((mod_rows - n_dec - 1, D_MODEL), F32)], axis=0)
    mod = _mod_call(cc, w_ada[0], b_ada)
    mod_lat = mod[:n_dec].reshape(n_dec, 1, N_MOD * D_MODEL)
    mod_ctx = mod[n_dec:n_dec + 1].reshape(1, 1, N_MOD * D_MODEL)

    lp = dict(
        w_norm1=w_norm1, w_in=w_in[0].astype(BF16),
        lam=(lambda_q1, lambda_k1, lambda_q2, lambda_k2), w_head_norm=w_head_norm,
        w_attn_proj=w_attn_proj[0].astype(BF16), w_conv_dw=w_conv_dw[0],
        conv_ln_g=conv_ln_g, conv_ln_b=conv_ln_b,
        w_conv_proj=w_conv_proj[0].astype(BF16), w_out=w_out[0].astype(BF16),
        w_norm2=w_norm2, w_up=w_up[0].astype(BF16), w_ffn_dw=w_ffn_dw[0],
        w_down=w_down[0].astype(BF16), w_final_norm=w_final_norm[None, :])

    y_prompt, (kf, vf) = _trunk_group(
        x_prompt, mod_ctx, False, False, None, lp, tm=seq_ctx, tq=seq_ctx, tk=seq_ctx)
    cache = (cache_k[:, 0].reshape(n_dec, past, ATTN_W),
             cache_v[:, 0].reshape(n_dec, past, ATTN_W))
    y_sample, _ = _trunk_group(
        x_sample, mod_lat, True, True, cache, lp, tm=512, tq=512, tk=2048)
    new_k = kf.reshape(n_ctx, 1, seq_ctx, N_HEADS, HEAD_W)
    new_v = vf.reshape(n_ctx, 1, seq_ctx, N_HEADS, V_DIM)
    return (y_prompt, y_sample, new_k, new_v)
```

```python
import functools

import jax
import jax.numpy as jnp
import numpy as np
from jax import lax
from jax.experimental import pallas as pl
from jax.experimental.pallas import tpu as pltpu

D_MODEL = 1024
N_HEADS = 4
QK_DIM = 64
V_DIM = 2 * QK_DIM
HEAD_W = 2 * QK_DIM
ATTN_W = N_HEADS * V_DIM
CONV_CH = D_MODEL // 2
DW_WIDTH = 31
D_FF = 2816
FFN_DW_WIDTH = 3
GRID_W = 64
ROPE_THETA = 10000.0
EPS = 1e-6
N_MOD = 6
LAM_INIT = 0.8 - 0.6 * float(np.exp(-0.3 * 0))
QK_SCALE = QK_DIM ** -0.5 * float(np.log2(np.e))

MAX_EXCESS = 16.0
LANES = 128
HALO = 16
CONV_PAD = (DW_WIDTH - 1) // 2
CONV_ROWS = 128
VMEM_LIMIT = 56 * 1024 * 1024

F32 = jnp.float32
BF16 = jnp.bfloat16


def _sigmoid(x):
    return 1.0 / (1.0 + jnp.exp(-x))


def _const_spec(shape):
    nd = len(shape)
    return pl.BlockSpec(shape, lambda *_: (0,) * nd, pipeline_mode=pl.Buffered(1))


def _params(n_axes):
    return pltpu.CompilerParams(
        dimension_semantics=("arbitrary",) * n_axes, vmem_limit_bytes=VMEM_LIMIT)


def _mod_kernel(c_ref, w_ref, b_ref, o_ref):
    c = c_ref[...]
    s = c * _sigmoid(c)
    w = w_ref[...]
    s_hi = s.astype(BF16)
    s_lo = (s - s_hi.astype(F32)).astype(BF16)
    w_hi = w.astype(BF16)
    w_lo = (w - w_hi.astype(F32)).astype(BF16)
    acc = jnp.dot(s_hi, w_hi, preferred_element_type=F32)
    acc += jnp.dot(s_hi, w_lo, preferred_element_type=F32)
    acc += jnp.dot(s_lo, w_hi, preferred_element_type=F32)
    o_ref[...] = acc + b_ref[...]


def _mod_call(cc, w_ada, b_ada):
    rows = cc.shape[0]
    n_out = w_ada.shape[1]
    bn = 1536
    return pl.pallas_call(
        _mod_kernel,
        grid=(n_out // bn,),
        in_specs=[
            pl.BlockSpec((rows, D_MODEL), lambda j: (0, 0)),
            pl.BlockSpec((D_MODEL, bn), lambda j: (0, j)),
            pl.BlockSpec((1, bn), lambda j: (0, j)),
        ],
        out_specs=pl.BlockSpec((rows, bn), lambda j: (0, j)),
        out_shape=jax.ShapeDtypeStruct((rows, n_out), F32),
        compiler_params=_params(1),
        name="mod",
    )(cc, w_ada, b_ada)


def _rope_tables(seq):
    t = np.arange(seq)
    row = (t // GRID_W).astype(np.float32).astype(np.float64)
    col = (t % GRID_W).astype(np.float32).astype(np.float64)
    half = QK_DIM // 2
    freqs = ROPE_THETA ** (-np.arange(0, half, 2, dtype=np.float64) / half)
    ar = row[:, None] * freqs
    ac = col[:, None] * freqs
    cos = np.concatenate([np.cos(ar), np.cos(ar), np.cos(ac), np.cos(ac)], axis=1)
    sin = np.concatenate([-np.sin(ar), np.sin(ar), -np.sin(ac), np.sin(ac)], axis=1)
    cos = np.tile(cos, (1, HEAD_W // QK_DIM)).astype(np.float32)
    sin = np.tile(sin, (1, HEAD_W // QK_DIM)).astype(np.float32)
    return jnp.asarray(cos), jnp.asarray(sin)


def _rope(x, cos, sin):
    quarter = QK_DIM // 4
    lane = lax.broadcasted_iota(jnp.int32, (1, HEAD_W), 1)
    fwd = pltpu.roll(x, HEAD_W - quarter, 1)
    bwd = pltpu.roll(x, quarter, 1)
    partner = jnp.where((lane & quarter) == 0, fwd, bwd)
    return x * cos + partner * sin


def _in_proj_kernel(*refs, use_rope, emit_f32_kv):
    it = iter(refs)
    x_ref, mod_ref, wn_ref, w_ref = next(it), next(it), next(it), next(it)
    cos_ref = sin_ref = None
    if use_rope:
        cos_ref, sin_ref = next(it), next(it)
    qt_ref, k_ref, vt_ref, glu_ref, gate_ref = next(it), next(it), next(it), next(it), next(it)
    kf_ref = vf_ref = None
    if emit_f32_kv:
        kf_ref, vf_ref = next(it), next(it)

    x = x_ref[...]
    shift = mod_ref[:, 0:D_MODEL]
    scale = mod_ref[:, D_MODEL:2 * D_MODEL]
    ms = jnp.mean(x * x, axis=-1, keepdims=True)
    h = x * lax.rsqrt(ms + EPS) * wn_ref[...]
    h = h * (1.0 + scale) + shift
    hb = h.astype(BF16)

    def proj(c0, c1):
        return jnp.dot(hb, w_ref[:, c0:c1], preferred_element_type=F32)

    pq = proj(0, ATTN_W)
    pk = proj(ATTN_W, 2 * ATTN_W)
    if emit_f32_kv:
        kf_ref[...] = pk
    for hd in range(N_HEADS):
        sl = slice(hd * HEAD_W, (hd + 1) * HEAD_W)
        qh = pq[:, sl]
        kh = pk[:, sl]
        if use_rope:
            cos = cos_ref[...]
            sin = sin_ref[...]
            qh = _rope(qh, cos, sin)
            kh = _rope(kh, cos, sin)
        qt_ref[sl, :] = (qh * QK_SCALE).T.astype(BF16)
        k_ref[:, sl] = kh.astype(BF16)

    pv = proj(2 * ATTN_W, 3 * ATTN_W)
    if emit_f32_kv:
        vf_ref[...] = pv
    for hd in range(N_HEADS):
        sl = slice(hd * HEAD_W, (hd + 1) * HEAD_W)
        vt_ref[sl, :] = pv[:, sl].T.astype(BF16)

    u0 = 3 * ATTN_W
    pu = proj(u0, u0 + 2 * CONV_CH)
    glu_ref[...] = (pu[:, :CONV_CH] * _sigmoid(pu[:, CONV_CH:])).astype(BF16)

    g0 = u0 + 2 * CONV_CH
    for j in range(2):
        pg = proj(g0 + j * D_MODEL, g0 + (j + 1) * D_MODEL)
        gate_ref[:, j * D_MODEL:(j + 1) * D_MODEL] = _sigmoid(pg).astype(BF16)


def _in_proj_call(x2d, mod3, w_norm1, w_in_bf, seq, tm, per_seq_mod, use_rope, emit_f32_kv):
    n_tok = x2d.shape[0]
    tps = seq // tm
    in_cols = w_in_bf.shape[1]
    mod_idx = (lambda i: (i // tps, 0, 0)) if per_seq_mod else (lambda i: (0, 0, 0))
    in_specs = [
        pl.BlockSpec((tm, D_MODEL), lambda i: (i, 0)),
        pl.BlockSpec((None, 1, N_MOD * D_MODEL), mod_idx),
        _const_spec((1, D_MODEL)),
        _const_spec((D_MODEL, in_cols)),
    ]
    args = [x2d, mod3, w_norm1, w_in_bf]
    if use_rope:
        cos, sin = _rope_tables(seq)
        in_specs += [pl.BlockSpec((tm, HEAD_W), lambda i: (i % tps, 0))] * 2
        args += [cos, sin]
    tok_spec = lambda w: pl.BlockSpec((tm, w), lambda i: (i, 0))
    tr_spec = pl.BlockSpec((ATTN_W, tm), lambda i: (0, i))
    tr_shape = jax.ShapeDtypeStruct((ATTN_W, n_tok), BF16)
    out_specs = [tr_spec, tok_spec(ATTN_W), tr_spec, tok_spec(CONV_CH), tok_spec(2 * D_MODEL)]
    out_shape = [tr_shape, jax.ShapeDtypeStruct((n_tok, ATTN_W), BF16), tr_shape,
                 jax.ShapeDtypeStruct((n_tok, CONV_CH), BF16),
                 jax.ShapeDtypeStruct((n_tok, 2 * D_MODEL), BF16)]
    if emit_f32_kv:
        out_specs += [tok_spec(ATTN_W)] * 2
        out_shape += [jax.ShapeDtypeStruct((n_tok, ATTN_W), F32)] * 2
    return pl.pallas_call(
        functools.partial(_in_proj_kernel, use_rope=use_rope, emit_f32_kv=emit_f32_kv),
        grid=(n_tok // tm,),
        in_specs=in_specs,
        out_specs=out_specs,
        out_shape=out_shape,
        compiler_params=_params(1),
        name="in_proj",
    )(*args)


def _attn_kernel(*refs, tq, tk, n_chunks, heads, use_cache):
    it = iter(refs)
    qt_ref, k_ref, vt_ref = next(it), next(it), next(it)
    ck_ref = cv_ref = None
    if use_cache:
        ck_ref, cv_ref = next(it), next(it)
    lq1, lk1, lq2, lk2, whn_ref = next(it), next(it), next(it), next(it), next(it)
    o_ref = next(it)
    for hd in range(heads):
        _attend_head(slice(hd * HEAD_W, (hd + 1) * HEAD_W), qt_ref, k_ref, vt_ref, ck_ref, cv_ref,
                     (lq1, lk1, lq2, lk2), whn_ref, o_ref, tq, tk, n_chunks)


def _attend_head(hs, qt_ref, k_ref, vt_ref, ck_ref, cv_ref, lam_refs, whn_ref, o_ref, tq, tk, n_chunks):
    lq1, lk1, lq2, lk2 = lam_refs
    qt = qt_ref[hs, :]
    zero = jnp.zeros((QK_DIM, tq), BF16)
    qs = jnp.concatenate(
        [jnp.concatenate([qt[0:QK_DIM, :], zero], axis=0),
         jnp.concatenate([zero, qt[QK_DIM:HEAD_W, :]], axis=0)], axis=1)

    chunks = [(k_ref[j * tk:(j + 1) * tk, hs], vt_ref[hs, j * tk:(j + 1) * tk])
              for j in range(n_chunks)]
    if ck_ref is not None:
        chunks.insert(0, (ck_ref[0, :, hs], cv_ref[0, :, hs].astype(F32).T.astype(BF16)))

    def finalize(l, acc):
        ot = acc / l
        lam = (jnp.exp(jnp.sum(lq1[...] * lk1[...], axis=-1, keepdims=True))
               - jnp.exp(jnp.sum(lq2[...] * lk2[...], axis=-1, keepdims=True)) + LAM_INIT)
        o = (ot[:, 0:tq] - lam * ot[:, tq:2 * tq]).T
        ms = jnp.mean(o * o, axis=-1, keepdims=True)
        o = o * lax.rsqrt(ms + EPS) * whn_ref[...] * (1.0 - LAM_INIT)
        o_ref[:, hs] = o.astype(BF16)

    kb, vtb = chunks[0]
    s = jnp.dot(kb, qs, preferred_element_type=F32)
    ref_row = jnp.max(s, axis=0, keepdims=True).astype(BF16).astype(F32)
    p = jnp.exp2(s - ref_row)
    l = jnp.sum(p, axis=0, keepdims=True)
    acc = jnp.dot(vtb, p.astype(BF16), preferred_element_type=F32)
    if len(chunks) == 1:
        finalize(l, acc)
        return

    row = lax.broadcasted_iota(jnp.int32, (16, 2 * tq), 0)
    neg_ref = jnp.where(row == 0, -ref_row, 0.0).astype(BF16)
    qs_aug = jnp.concatenate(
        [qs, neg_ref, jnp.zeros((HEAD_W - 16, 2 * tq), BF16)], axis=0)
    excess = jnp.zeros((1, 2 * tq), F32)
    for kb, vtb in chunks[1:]:
        k_aug = jnp.concatenate([kb, jnp.ones((kb.shape[0], HEAD_W), BF16)], axis=1)
        s = jnp.dot(k_aug, qs_aug, preferred_element_type=F32)
        p = jnp.exp2(s)
        excess = jnp.maximum(excess, jnp.max(s, axis=0, keepdims=True))
        l = l + jnp.sum(p, axis=0, keepdims=True)
        acc = acc + jnp.dot(vtb, p.astype(BF16), preferred_element_type=F32)
    in_range = jnp.max(excess) <= MAX_EXCESS

    @pl.when(in_range)
    def _():
        finalize(l, acc)

    @pl.when(jnp.logical_not(in_range))
    def _():
        def update(state, kb, vtb):
            m_prev, l_prev, acc = state
            s = jnp.dot(kb, qs, preferred_element_type=F32)
            m_new = jnp.maximum(m_prev, jnp.max(s, axis=0, keepdims=True))
            alpha = jnp.exp2(m_prev - m_new)
            p = jnp.exp2(s - m_new)
            l_new = alpha * l_prev + jnp.sum(p, axis=0, keepdims=True)
            acc = alpha * acc + jnp.dot(vtb, p.astype(BF16), preferred_element_type=F32)
            return m_new, l_new, acc

        state = (jnp.full((1, 2 * tq), -jnp.inf, F32), jnp.zeros((1, 2 * tq), F32),
                 jnp.zeros((V_DIM, 2 * tq), F32))
        for kb, vtb in chunks:
            state = update(state, kb, vtb)
        finalize(state[1], state[2])


def _attn_call(qt, k, vt, cache_k, cache_v, lam_params, w_head_norm, n_seq, seq, tq, tk, heads):
    n_tok = k.shape[0]
    qps = seq // tq
    use_cache = cache_k is not None
    hw = heads * HEAD_W
    in_specs = [
        pl.BlockSpec((hw, tq), lambda b, h, i: (h, b * qps + i)),
        pl.BlockSpec((seq, hw), lambda b, h, i: (b, h)),
        pl.BlockSpec((hw, seq), lambda b, h, i: (h, b)),
    ]
    args = [qt, k, vt]
    if use_cache:
        past = cache_k.shape[1]
        in_specs += [pl.BlockSpec((1, past, hw), lambda b, h, i: (b, 0, h))] * 2
        args += [cache_k, cache_v]
    in_specs += [_const_spec((1, QK_DIM))] * 4 + [_const_spec((1, V_DIM))]
    args += list(lam_params) + [w_head_norm]
    return pl.pallas_call(
        functools.partial(_attn_kernel, tq=tq, tk=tk, n_chunks=seq // tk, heads=heads,
                          use_cache=use_cache),
        grid=(n_seq, N_HEADS // heads, qps),
        in_specs=in_specs,
        out_specs=pl.BlockSpec((tq, hw), lambda b, h, i: (b * qps + i, h)),
        out_shape=jax.ShapeDtypeStruct((n_tok, ATTN_W), BF16),
        compiler_params=_params(3),
        name="attn",
    )(*args)


def _halo_specs(tm, width, n_tok):
    r = tm // HALO
    last = n_tok // HALO - 1
    return [
        pl.BlockSpec((tm, width), lambda i: (i, 0)),
        pl.BlockSpec((HALO, width), lambda i: (jnp.maximum(i * r - 1, 0), 0)),
        pl.BlockSpec((HALO, width), lambda i: (jnp.minimum((i + 1) * r, last), 0)),
    ]


def _fill_ext(ext_ref, main_ref, prev_ref, next_ref, tm, tps):
    j = pl.program_id(0) % tps
    prev = prev_ref[...].astype(ext_ref.dtype)
    nxt = next_ref[...].astype(ext_ref.dtype)
    ext_ref[0:HALO, :] = jnp.where(j > 0, prev, jnp.zeros_like(prev))
    ext_ref[HALO:HALO + tm, :] = main_ref[...].astype(ext_ref.dtype)
    ext_ref[HALO + tm:, :] = jnp.where(j < tps - 1, nxt, jnp.zeros_like(nxt))


def _merge_kernel(o_ref, glu_ref, glu_prev, glu_next, gate_ref, x_ref, mod_ref,
                  wap_ref, wdw_ref, lng_ref, lnb_ref, wcp_ref, wout_ref, wn2_ref,
                  x1_ref, h2_ref, ext_ref, conv_ref, *, tm, tps):
    j = pl.program_id(0) % tps
    for s in range(CONV_CH // LANES):
        cols = slice(s * LANES, (s + 1) * LANES)
        prev = glu_prev[:, cols].astype(F32)
        nxt = glu_next[:, cols].astype(F32)
        ext_ref[s, 0:HALO, :] = jnp.where(j > 0, prev, jnp.zeros_like(prev))
        ext_ref[s, HALO:HALO + tm, :] = glu_ref[:, cols].astype(F32)
        ext_ref[s, HALO + tm:, :] = jnp.where(j < tps - 1, nxt, jnp.zeros_like(nxt))
    for s in range(CONV_CH // LANES):
        cols = slice(s * LANES, (s + 1) * LANES)
        for r0 in range(0, tm, CONV_ROWS):
            acc = None
            for t in range(DW_WIDTH):
                tap = (ext_ref[s, pl.ds(HALO - CONV_PAD + r0 + t, CONV_ROWS, stride=1), :]
                       * wdw_ref[t:t + 1, cols])
                acc = tap if acc is None else acc + tap
            conv_ref[r0:r0 + CONV_ROWS, cols] = acc
    acc = conv_ref[...]
    mu = jnp.mean(acc, axis=-1, keepdims=True)
    d = acc - mu
    var = jnp.mean(d * d, axis=-1, keepdims=True)
    y = d * lax.rsqrt(var + EPS) * lng_ref[...] + lnb_ref[...]
    cv = (y * _sigmoid(y)).astype(BF16)
    conv_out = jnp.dot(cv, wcp_ref[...], preferred_element_type=F32)
    attn_out = jnp.dot(o_ref[...], wap_ref[...], preferred_element_type=F32)
    merged = (gate_ref[:, 0:D_MODEL].astype(F32) * attn_out
              + gate_ref[:, D_MODEL:2 * D_MODEL].astype(F32) * conv_out)
    mix = jnp.dot(merged.astype(BF16), wout_ref[...], preferred_element_type=F32)
    gate1 = mod_ref[:, 2 * D_MODEL:3 * D_MODEL]
    shift2 = mod_ref[:, 3 * D_MODEL:4 * D_MODEL]
    scale2 = mod_ref[:, 4 * D_MODEL:5 * D_MODEL]
    x1 = x_ref[...] + gate1 * mix
    x1_ref[...] = x1
    ms = jnp.mean(x1 * x1, axis=-1, keepdims=True)
    h2 = x1 * lax.rsqrt(ms + EPS) * wn2_ref[...]
    h2_ref[...] = (h2 * (1.0 + scale2) + shift2).astype(BF16)


def _merge_call(o_n, glu, gates, x2d, mod3, w_attn_proj, w_conv_dw, ln_g, ln_b,
                w_conv_proj, w_out, w_norm2, seq, tm, per_seq_mod):
    n_tok = x2d.shape[0]
    tps = seq // tm
    mod_idx = (lambda i: (i // tps, 0, 0)) if per_seq_mod else (lambda i: (0, 0, 0))
    tok_spec = lambda w: pl.BlockSpec((tm, w), lambda i: (i, 0))
    in_specs = (
        [tok_spec(ATTN_W)] + _halo_specs(tm, CONV_CH, n_tok)
        + [tok_spec(2 * D_MODEL), tok_spec(D_MODEL),
           pl.BlockSpec((None, 1, N_MOD * D_MODEL), mod_idx),
           _const_spec((ATTN_W, D_MODEL)), _const_spec((DW_WIDTH, CONV_CH)),
           _const_spec((1, CONV_CH)), _const_spec((1, CONV_CH)),
           _const_spec((CONV_CH, D_MODEL)), _const_spec((D_MODEL, D_MODEL)),
           _const_spec((1, D_MODEL))])
    return pl.pallas_call(
        functools.partial(_merge_kernel, tm=tm, tps=tps),
        grid=(n_tok // tm,),
        in_specs=in_specs,
        out_specs=[tok_spec(D_MODEL), tok_spec(D_MODEL)],
        out_shape=[jax.ShapeDtypeStruct((n_tok, D_MODEL), F32),
                   jax.ShapeDtypeStruct((n_tok, D_MODEL), BF16)],
        scratch_shapes=[pltpu.VMEM((CONV_CH // LANES, tm + 2 * HALO, LANES), F32),
                        pltpu.VMEM((tm, CONV_CH), F32)],
        compiler_params=_params(1),
        name="merge",
    )(o_n, glu, glu, glu, gates, x2d, mod3, w_attn_proj, w_conv_dw, ln_g, ln_b,
      w_conv_proj, w_out, w_norm2)


def _ffn_kernel(h_ref, h_prev, h_next, x1_ref, mod_ref, wup_ref, wdw_ref, wdn_ref, wfn_ref,
                y_ref, lhs_ref, u_ref, acc_ref, *, tm, tps, cn):
    _fill_ext(lhs_ref, h_ref, h_prev, h_next, tm, tps)
    lhs = lhs_ref[...]
    n_slabs = cn // LANES

    def up_proj(par, half, c0):
        res = jnp.dot(lhs, wup_ref[:, c0:c0 + cn], preferred_element_type=F32)
        for s in range(n_slabs):
            u_ref[par, half, s] = res[:, s * LANES:(s + 1) * LANES]

    def conv(par, half, c0):
        outs = []
        for s in range(n_slabs):
            cols = slice(c0 + s * LANES, c0 + (s + 1) * LANES)
            taps = [u_ref[par, half, s, pl.ds(HALO - 1 + j, tm, stride=1), :] * wdw_ref[j:j + 1, cols]
                    for j in range(FFN_DW_WIDTH)]
            outs.append(taps[0] + taps[1] + taps[2])
        return jnp.concatenate(outs, axis=1)

    n_chunks = D_FF // cn
    up_proj(0, 0, 0)
    up_proj(0, 1, D_FF)
    for c in range(n_chunks):
        a0 = c * cn
        b0 = D_FF + c * cn
        par = c % 2
        if c + 1 < n_chunks:
            up_proj(1 - par, 0, a0 + cn)
            up_proj(1 - par, 1, b0 + cn)
        a = conv(par, 0, a0)
        b = conv(par, 1, b0)
        act = (a * _sigmoid(a) * b).astype(BF16)
        down = jnp.dot(act, wdn_ref[a0:a0 + cn, :], preferred_element_type=F32)
        if c == 0:
            acc_ref[...] = down
        else:
            acc_ref[...] += down

    gate2 = mod_ref[:, 5 * D_MODEL:6 * D_MODEL]
    y = x1_ref[...] + gate2 * acc_ref[...]
    ms = jnp.mean(y * y, axis=-1, keepdims=True)
    y_ref[...] = y * lax.rsqrt(ms + EPS) * wfn_ref[...]


def _ffn_call(h2, x1, mod3, w_up, w_ffn_dw, w_down, w_final_norm, seq, tm, per_seq_mod):
    n_tok = x1.shape[0]
    tps = seq // tm
    cn = 256
    mod_idx = (lambda i: (i // tps, 0, 0)) if per_seq_mod else (lambda i: (0, 0, 0))
    tok_spec = lambda w: pl.BlockSpec((tm, w), lambda i: (i, 0))
    in_specs = (
        _halo_specs(tm, D_MODEL, n_tok)
        + [tok_spec(D_MODEL), pl.BlockSpec((None, 1, N_MOD * D_MODEL), mod_idx),
           _const_spec((D_MODEL, 2 * D_FF)), _const_spec((FFN_DW_WIDTH, 2 * D_FF)),
           _const_spec((D_FF, D_MODEL)), _const_spec((1, D_MODEL))])
    return pl.pallas_call(
        functools.partial(_ffn_kernel, tm=tm, tps=tps, cn=cn),
        grid=(n_tok // tm,),
        in_specs=in_specs,
        out_specs=tok_spec(D_MODEL),
        out_shape=jax.ShapeDtypeStruct((n_tok, D_MODEL), F32),
        scratch_shapes=[
            pltpu.VMEM((tm + 2 * HALO, D_MODEL), BF16),
            pltpu.VMEM((2, 2, cn // LANES, tm + 2 * HALO, LANES), F32),
            pltpu.VMEM((tm, D_MODEL), F32),
        ],
        compiler_params=_params(1),
        name="ffn",
    )(h2, h2, h2, x1, mod3, w_up, w_ffn_dw, w_down, w_final_norm)


def _tiles(seq):
    tm = min(seq, 512)
    tq = min(seq, 512)
    tk = min(seq, 2048)
    heads = N_HEADS if seq <= 256 else 1
    return tm, tq, tk, heads


def _trunk_group(x, mod3, per_seq_mod, use_rope, cache, lp):
    n_seq, seq, _ = x.shape
    tm, tq, tk, heads = _tiles(seq)
    x2d = x.reshape(n_seq * seq, D_MODEL)
    emit_f32_kv = cache is None
    outs = _in_proj_call(x2d, mod3, lp["w_norm1"], lp["w_in"], seq, tm,
                         per_seq_mod, use_rope, emit_f32_kv)
    q, k, v, glu, gates = outs[:5]
    cache_k = cache_v = None
    if cache is not None:
        cache_k, cache_v = cache
    o_n = _attn_call(q, k, v, cache_k, cache_v, lp["lam"], lp["w_head_norm"],
                     n_seq, seq, tq, tk, heads)
    x1, h2 = _merge_call(o_n, glu, gates, x2d, mod3, lp["w_attn_proj"], lp["w_conv_dw"],
                         lp["conv_ln_g"], lp["conv_ln_b"], lp["w_conv_proj"], lp["w_out"],
                         lp["w_norm2"], seq, tm, per_seq_mod)
    y = _ffn_call(h2, x1, mod3, lp["w_up"], lp["w_ffn_dw"], lp["w_down"],
                  lp["w_final_norm"], seq, tm, per_seq_mod)
    return y.reshape(n_seq, seq, D_MODEL), outs[5:]


def kernel(x_prompt, x_sample, cache_k, cache_v, c, c_ctx, w_ada, b_ada, w_norm1, w_in, lambda_q1, lambda_k1, lambda_q2, lambda_k2, w_head_norm, w_attn_proj, w_conv_dw, conv_ln_g, conv_ln_b, w_conv_proj, w_out, w_norm2, w_up, w_ffn_dw, w_down, w_final_norm):
    assert w_in.shape[0] == 1, "single trunk layer"
    n_dec = x_sample.shape[0]
    n_ctx, seq_ctx = x_prompt.shape[0], x_prompt.shape[1]
    past = cache_k.shape[2]

    mod_rows = 16
    cc = jnp.concatenate(
        [c, c_ctx[None, :], jnp.zeros((mod_rows - n_dec - 1, D_MODEL), F32)], axis=0)
    mod = _mod_call(cc, w_ada[0], b_ada)
    mod_lat = mod[:n_dec].reshape(n_dec, 1, N_MOD * D_MODEL)
    mod_ctx = mod[n_dec:n_dec + 1].reshape(1, 1, N_MOD * D_MODEL)

    lp = dict(
        w_norm1=w_norm1, w_in=w_in[0].astype(BF16),
        lam=(lambda_q1, lambda_k1, lambda_q2, lambda_k2), w_head_norm=w_head_norm,
        w_attn_proj=w_attn_proj[0].astype(BF16), w_conv_dw=w_conv_dw[0],
        conv_ln_g=conv_ln_g, conv_ln_b=conv_ln_b,
        w_conv_proj=w_conv_proj[0].astype(BF16), w_out=w_out[0].astype(BF16),
        w_norm2=w_norm2, w_up=w_up[0].astype(BF16), w_ffn_dw=w_ffn_dw[0],
        w_down=w_down[0].astype(BF16), w_final_norm=w_final_norm[None, :])

    y_prompt, (kf, vf) = _trunk_group(x_prompt, mod_ctx, False, False, None, lp)
    cache = (cache_k[:, 0].astype(BF16).reshape(n_dec, past, ATTN_W),
             cache_v[:, 0].astype(BF16).reshape(n_dec, past, ATTN_W))
    y_sample, _ = _trunk_group(x_sample, mod_lat, True, True, cache, lp)
    new_k = kf.reshape(n_ctx, 1, seq_ctx, N_HEADS, HEAD_W)
    new_v = vf.reshape(n_ctx, 1, seq_ctx, N_HEADS, V_DIM)
    return (y_prompt, y_sample, new_k, new_v)
```

```python
import functools

import jax
import jax.numpy as jnp
import numpy as np
from jax import lax
from jax.experimental import pallas as pl
from jax.experimental.pallas import tpu as pltpu

D_MODEL = 1024
N_HEADS = 4
QK_DIM = 64
V_DIM = 2 * QK_DIM
HEAD_W = 2 * QK_DIM
ATTN_W = N_HEADS * V_DIM
CONV_CH = D_MODEL // 2
DW_WIDTH = 31
D_FF = 2816
FFN_DW_WIDTH = 3
GRID_W = 64
ROPE_THETA = 10000.0
EPS = 1e-6
N_MOD = 6
LAM_INIT = 0.8 - 0.6 * float(np.exp(-0.3 * 0))
QK_SCALE = QK_DIM ** -0.5 * float(np.log2(np.e))

MAX_EXCESS = 16.0
LANES = 128
HALO = 16
CONV_PAD = (DW_WIDTH - 1) // 2
CONV_ROWS = 128
VMEM_LIMIT = 56 * 1024 * 1024

F32 = jnp.float32
BF16 = jnp.bfloat16


def _sigmoid(x):
    return 1.0 / (1.0 + jnp.exp(-x))


def _const_spec(shape):
    nd = len(shape)
    return pl.BlockSpec(shape, lambda *_: (0,) * nd, pipeline_mode=pl.Buffered(1))


def _params(n_axes):
    return pltpu.CompilerParams(
        dimension_semantics=("arbitrary",) * n_axes, vmem_limit_bytes=VMEM_LIMIT)


def _mod_kernel(c_ref, w_ref, b_ref, o_ref):
    c = c_ref[...]
    s = c * _sigmoid(c)
    w = w_ref[...]
    s_hi = s.astype(BF16)
    s_lo = (s - s_hi.astype(F32)).astype(BF16)
    w_hi = w.astype(BF16)
    w_lo = (w - w_hi.astype(F32)).astype(BF16)
    acc = jnp.dot(s_hi, w_hi, preferred_element_type=F32)
    acc += jnp.dot(s_hi, w_lo, preferred_element_type=F32)
    acc += jnp.dot(s_lo, w_hi, preferred_element_type=F32)
    o_ref[...] = acc + b_ref[...]


def _mod_call(cc, w_ada, b_ada):
    rows = cc.shape[0]
    n_out = w_ada.shape[1]
    bn = 1536
    return pl.pallas_call(
        _mod_kernel,
        grid=(n_out // bn,),
        in_specs=[
            pl.BlockSpec((rows, D_MODEL), lambda j: (0, 0)),
            pl.BlockSpec((D_MODEL, bn), lambda j: (0, j)),
            pl.BlockSpec((1, bn), lambda j: (0, j)),
        ],
        out_specs=pl.BlockSpec((rows, bn), lambda j: (0, j)),
        out_shape=jax.ShapeDtypeStruct((rows, n_out), F32),
        compiler_params=_params(1),
        name="mod",
    )(cc, w_ada, b_ada)


def _rope_tables(seq):
    t = np.arange(seq)
    row = (t // GRID_W).astype(np.float32).astype(np.float64)
    col = (t % GRID_W).astype(np.float32).astype(np.float64)
    half = QK_DIM // 2
    freqs = ROPE_THETA ** (-np.arange(0, half, 2, dtype=np.float64) / half)
    ar = row[:, None] * freqs
    ac = col[:, None] * freqs
    cos = np.concatenate([np.cos(ar), np.cos(ar), np.cos(ac), np.cos(ac)], axis=1)
    sin = np.concatenate([-np.sin(ar), np.sin(ar), -np.sin(ac), np.sin(ac)], axis=1)
    cos = np.tile(cos, (1, HEAD_W // QK_DIM)).astype(np.float32)
    sin = np.tile(sin, (1, HEAD_W // QK_DIM)).astype(np.float32)
    return jnp.asarray(cos), jnp.asarray(sin)


def _rope(x, cos, sin):
    quarter = QK_DIM // 4
    lane = lax.broadcasted_iota(jnp.int32, (1, HEAD_W), 1)
    fwd = pltpu.roll(x, HEAD_W - quarter, 1)
    bwd = pltpu.roll(x, quarter, 1)
    partner = jnp.where((lane & quarter) == 0, fwd, bwd)
    return x * cos + partner * sin


def _in_proj_kernel(*refs, use_rope, emit_f32_kv):
    it = iter(refs)
    x_ref, mod_ref, wn_ref, w_ref = next(it), next(it), next(it), next(it)
    cos_ref = sin_ref = None
    if use_rope:
        cos_ref, sin_ref = next(it), next(it)
    qt_ref, k_ref, vt_ref, glu_ref, gate_ref = next(it), next(it), next(it), next(it), next(it)
    kf_ref = vf_ref = None
    if emit_f32_kv:
        kf_ref, vf_ref = next(it), next(it)

    x = x_ref[...]
    shift = mod_ref[:, 0:D_MODEL]
    scale = mod_ref[:, D_MODEL:2 * D_MODEL]
    ms = jnp.mean(x * x, axis=-1, keepdims=True)
    h = x * lax.rsqrt(ms + EPS) * wn_ref[...]
    h = h * (1.0 + scale) + shift
    hb = h.astype(BF16)

    def proj(c0, c1):
        return jnp.dot(hb, w_ref[:, c0:c1], preferred_element_type=F32)

    pq = proj(0, ATTN_W)
    pk = proj(ATTN_W, 2 * ATTN_W)
    if emit_f32_kv:
        kf_ref[...] = pk
    for hd in range(N_HEADS):
        sl = slice(hd * HEAD_W, (hd + 1) * HEAD_W)
        qh = pq[:, sl]
        kh = pk[:, sl]
        if use_rope:
            cos = cos_ref[...]
            sin = sin_ref[...]
            qh = _rope(qh, cos, sin)
            kh = _rope(kh, cos, sin)
        qt_ref[sl, :] = (qh * QK_SCALE).T.astype(BF16)
        k_ref[:, sl] = kh.astype(BF16)

    pv = proj(2 * ATTN_W, 3 * ATTN_W)
    if emit_f32_kv:
        vf_ref[...] = pv
    for hd in range(N_HEADS):
        sl = slice(hd * HEAD_W, (hd + 1) * HEAD_W)
        vt_ref[sl, :] = pv[:, sl].T.astype(BF16)

    u0 = 3 * ATTN_W
    pu = proj(u0, u0 + 2 * CONV_CH)
    glu_ref[...] = (pu[:, :CONV_CH] * _sigmoid(pu[:, CONV_CH:])).astype(BF16)

    g0 = u0 + 2 * CONV_CH
    for j in range(2):
        pg = proj(g0 + j * D_MODEL, g0 + (j + 1) * D_MODEL)
        gate_ref[:, j * D_MODEL:(j + 1) * D_MODEL] = _sigmoid(pg).astype(BF16)


def _in_proj_call(x2d, mod3, w_norm1, w_in_bf, seq, tm, per_seq_mod, use_rope, emit_f32_kv):
    n_tok = x2d.shape[0]
    tps = seq // tm
    in_cols = w_in_bf.shape[1]
    mod_idx = (lambda i: (i // tps, 0, 0)) if per_seq_mod else (lambda i: (0, 0, 0))
    in_specs = [
        pl.BlockSpec((tm, D_MODEL), lambda i: (i, 0)),
        pl.BlockSpec((None, 1, N_MOD * D_MODEL), mod_idx),
        _const_spec((1, D_MODEL)),
        _const_spec((D_MODEL, in_cols)),
    ]
    args = [x2d, mod3, w_norm1, w_in_bf]
    if use_rope:
        cos, sin = _rope_tables(seq)
        in_specs += [pl.BlockSpec((tm, HEAD_W), lambda i: (i % tps, 0))] * 2
        args += [cos, sin]
    tok_spec = lambda w: pl.BlockSpec((tm, w), lambda i: (i, 0))
    tr_spec = pl.BlockSpec((ATTN_W, tm), lambda i: (0, i))
    tr_shape = jax.ShapeDtypeStruct((ATTN_W, n_tok), BF16)
    out_specs = [tr_spec, tok_spec(ATTN_W), tr_spec, tok_spec(CONV_CH), tok_spec(2 * D_MODEL)]
    out_shape = [tr_shape, jax.ShapeDtypeStruct((n_tok, ATTN_W), BF16), tr_shape,
                 jax.ShapeDtypeStruct((n_tok, CONV_CH), BF16),
                 jax.ShapeDtypeStruct((n_tok, 2 * D_MODEL), BF16)]
    if emit_f32_kv:
        out_specs += [tok_spec(ATTN_W)] * 2
        out_shape += [jax.ShapeDtypeStruct((n_tok, ATTN_W), F32)] * 2
    return pl.pallas_call(
        functools.partial(_in_proj_kernel, use_rope=use_rope, emit_f32_kv=emit_f32_kv),
        grid=(n_tok // tm,),
        in_specs=in_specs,
        out_specs=out_specs,
        out_shape=out_shape,
        compiler_params=_params(1),
        name="in_proj",
    )(*args)


def _attn_kernel(*refs, tq, tk, n_chunks, heads, use_cache):
    it = iter(refs)
    qt_ref, k_ref, vt_ref = next(it), next(it), next(it)
    ck_ref = cv_ref = None
    if use_cache:
        ck_ref, cv_ref = next(it), next(it)
    lq1, lk1, lq2, lk2, whn_ref = next(it), next(it), next(it), next(it), next(it)
    o_ref = next(it)
    for hd in range(heads):
        _attend_head(hd, qt_ref, k_ref, vt_ref, ck_ref, cv_ref,
                     (lq1, lk1, lq2, lk2), whn_ref, o_ref, tq, tk, n_chunks)


def _attend_head(hd, qt_ref, k_ref, vt_ref, ck_ref, cv_ref, lam_refs, whn_ref, o_ref, tq, tk, n_chunks):
    hs = slice(hd * HEAD_W, (hd + 1) * HEAD_W)
    lq1, lk1, lq2, lk2 = lam_refs
    qt = qt_ref[hs, :]
    zero = jnp.zeros((QK_DIM, tq), BF16)
    qs = jnp.concatenate(
        [jnp.concatenate([qt[0:QK_DIM, :], zero], axis=0),
         jnp.concatenate([zero, qt[QK_DIM:HEAD_W, :]], axis=0)], axis=1)

    chunks = [(k_ref[j * tk:(j + 1) * tk, hs], vt_ref[hs, j * tk:(j + 1) * tk])
              for j in range(n_chunks)]
    if ck_ref is not None:
        chunks.insert(0, (ck_ref[0, :, hd, :].astype(BF16), cv_ref[0, :, hd, :].T.astype(BF16)))

    def finalize(l, acc):
        ot = acc / l
        lam = (jnp.exp(jnp.sum(lq1[...] * lk1[...], axis=-1, keepdims=True))
               - jnp.exp(jnp.sum(lq2[...] * lk2[...], axis=-1, keepdims=True)) + LAM_INIT)
        o = (ot[:, 0:tq] - lam * ot[:, tq:2 * tq]).T
        ms = jnp.mean(o * o, axis=-1, keepdims=True)
        o = o * lax.rsqrt(ms + EPS) * whn_ref[...] * (1.0 - LAM_INIT)
        o_ref[:, hs] = o.astype(BF16)

    kb, vtb = chunks[0]
    s = jnp.dot(kb, qs, preferred_element_type=F32)
    ref_row = jnp.max(s, axis=0, keepdims=True).astype(BF16).astype(F32)
    p = jnp.exp2(s - ref_row)
    l = jnp.sum(p, axis=0, keepdims=True)
    acc = jnp.dot(vtb, p.astype(BF16), preferred_element_type=F32)
    if len(chunks) == 1:
        finalize(l, acc)
        return

    row = lax.broadcasted_iota(jnp.int32, (16, 2 * tq), 0)
    neg_ref = jnp.where(row == 0, -ref_row, 0.0).astype(BF16)
    qs_aug = jnp.concatenate(
        [qs, neg_ref, jnp.zeros((HEAD_W - 16, 2 * tq), BF16)], axis=0)
    excess = jnp.zeros((1, 2 * tq), F32)
    for kb, vtb in chunks[1:]:
        k_aug = jnp.concatenate([kb, jnp.ones((kb.shape[0], HEAD_W), BF16)], axis=1)
        s = jnp.dot(k_aug, qs_aug, preferred_element_type=F32)
        p = jnp.exp2(s)
        excess = jnp.maximum(excess, jnp.max(s, axis=0, keepdims=True))
        l = l + jnp.sum(p, axis=0, keepdims=True)
        acc = acc + jnp.dot(vtb, p.astype(BF16), preferred_element_type=F32)
    in_range = jnp.max(excess) <= MAX_EXCESS

    @pl.when(in_range)
    def _():
        finalize(l, acc)

    @pl.when(jnp.logical_not(in_range))
    def _():
        def update(state, kb, vtb):
            m_prev, l_prev, acc = state
            s = jnp.dot(kb, qs, preferred_element_type=F32)
            m_new = jnp.maximum(m_prev, jnp.max(s, axis=0, keepdims=True))
            alpha = jnp.exp2(m_prev - m_new)
            p = jnp.exp2(s - m_new)
            l_new = alpha * l_prev + jnp.sum(p, axis=0, keepdims=True)
            acc = alpha * acc + jnp.dot(vtb, p.astype(BF16), preferred_element_type=F32)
            return m_new, l_new, acc

        state = (jnp.full((1, 2 * tq), -jnp.inf, F32), jnp.zeros((1, 2 * tq), F32),
                 jnp.zeros((V_DIM, 2 * tq), F32))
        for kb, vtb in chunks:
            state = update(state, kb, vtb)
        finalize(state[1], state[2])


def _attn_call(qt, k, vt, cache_k, cache_v, lam_params, w_head_norm, n_seq, seq, tq, tk, heads):
    n_tok = k.shape[0]
    qps = seq // tq
    use_cache = cache_k is not None
    hw = heads * HEAD_W
    in_specs = [
        pl.BlockSpec((hw, tq), lambda b, h, i: (h, b * qps + i)),
        pl.BlockSpec((seq, hw), lambda b, h, i: (b, h)),
        pl.BlockSpec((hw, seq), lambda b, h, i: (h, b)),
    ]
    args = [qt, k, vt]
    if use_cache:
        past = cache_k.shape[1]
        assert heads == N_HEADS, "cache blocks carry all heads"
        in_specs += [pl.BlockSpec((1, past, N_HEADS, HEAD_W), lambda b, h, i: (b, 0, 0, 0))] * 2
        args += [cache_k, cache_v]
    in_specs += [_const_spec((1, QK_DIM))] * 4 + [_const_spec((1, V_DIM))]
    args += list(lam_params) + [w_head_norm]
    return pl.pallas_call(
        functools.partial(_attn_kernel, tq=tq, tk=tk, n_chunks=seq // tk, heads=heads,
                          use_cache=use_cache),
        grid=(n_seq, N_HEADS // heads, qps),
        in_specs=in_specs,
        out_specs=pl.BlockSpec((tq, hw), lambda b, h, i: (b * qps + i, h)),
        out_shape=jax.ShapeDtypeStruct((n_tok, ATTN_W), BF16),
        compiler_params=_params(3),
        name="attn",
    )(*args)


def _halo_specs(tm, width, n_tok):
    r = tm // HALO
    last = n_tok // HALO - 1
    return [
        pl.BlockSpec((tm, width), lambda i: (i, 0)),
        pl.BlockSpec((HALO, width), lambda i: (jnp.maximum(i * r - 1, 0), 0)),
        pl.BlockSpec((HALO, width), lambda i: (jnp.minimum((i + 1) * r, last), 0)),
    ]


def _fill_ext(ext_ref, main_ref, prev_ref, next_ref, tm, tps):
    j = pl.program_id(0) % tps
    prev = prev_ref[...].astype(ext_ref.dtype)
    nxt = next_ref[...].astype(ext_ref.dtype)
    ext_ref[0:HALO, :] = jnp.where(j > 0, prev, jnp.zeros_like(prev))
    ext_ref[HALO:HALO + tm, :] = main_ref[...].astype(ext_ref.dtype)
    ext_ref[HALO + tm:, :] = jnp.where(j < tps - 1, nxt, jnp.zeros_like(nxt))


def _merge_kernel(o_ref, glu_ref, glu_prev, glu_next, gate_ref, x_ref, mod_ref,
                  wap_ref, wdw_ref, lng_ref, lnb_ref, wcp_ref, wout_ref, wn2_ref,
                  x1_ref, h2_ref, ext_ref, conv_ref, *, tm, tps):
    j = pl.program_id(0) % tps
    for s in range(CONV_CH // LANES):
        cols = slice(s * LANES, (s + 1) * LANES)
        prev = glu_prev[:, cols].astype(F32)
        nxt = glu_next[:, cols].astype(F32)
        ext_ref[s, 0:HALO, :] = jnp.where(j > 0, prev, jnp.zeros_like(prev))
        ext_ref[s, HALO:HALO + tm, :] = glu_ref[:, cols].astype(F32)
        ext_ref[s, HALO + tm:, :] = jnp.where(j < tps - 1, nxt, jnp.zeros_like(nxt))
    for s in range(CONV_CH // LANES):
        cols = slice(s * LANES, (s + 1) * LANES)
        for r0 in range(0, tm, CONV_ROWS):
            acc = None
            for t in range(DW_WIDTH):
                tap = (ext_ref[s, pl.ds(HALO - CONV_PAD + r0 + t, CONV_ROWS, stride=1), :]
                       * wdw_ref[t:t + 1, cols])
                acc = tap if acc is None else acc + tap
            conv_ref[r0:r0 + CONV_ROWS, cols] = acc
    acc = conv_ref[...]
    mu = jnp.mean(acc, axis=-1, keepdims=True)
    d = acc - mu
    var = jnp.mean(d * d, axis=-1, keepdims=True)
    y = d * lax.rsqrt(var + EPS) * lng_ref[...] + lnb_ref[...]
    cv = (y * _sigmoid(y)).astype(BF16)
    conv_out = jnp.dot(cv, wcp_ref[...], preferred_element_type=F32)
    attn_out = jnp.dot(o_ref[...], wap_ref[...], preferred_element_type=F32)
    merged = (gate_ref[:, 0:D_MODEL].astype(F32) * attn_out
              + gate_ref[:, D_MODEL:2 * D_MODEL].astype(F32) * conv_out)
    mix = jnp.dot(merged.astype(BF16), wout_ref[...], preferred_element_type=F32)
    gate1 = mod_ref[:, 2 * D_MODEL:3 * D_MODEL]
    shift2 = mod_ref[:, 3 * D_MODEL:4 * D_MODEL]
    scale2 = mod_ref[:, 4 * D_MODEL:5 * D_MODEL]
    x1 = x_ref[...] + gate1 * mix
    x1_ref[...] = x1
    ms = jnp.mean(x1 * x1, axis=-1, keepdims=True)
    h2 = x1 * lax.rsqrt(ms + EPS) * wn2_ref[...]
    h2_ref[...] = (h2 * (1.0 + scale2) + shift2).astype(BF16)


def _merge_call(o_n, glu, gates, x2d, mod3, w_attn_proj, w_conv_dw, ln_g, ln_b,
                w_conv_proj, w_out, w_norm2, seq, tm, per_seq_mod):
    n_tok = x2d.shape[0]
    tps = seq // tm
    mod_idx = (lambda i: (i // tps, 0, 0)) if per_seq_mod else (lambda i: (0, 0, 0))
    tok_spec = lambda w: pl.BlockSpec((tm, w), lambda i: (i, 0))
    in_specs = (
        [tok_spec(ATTN_W)] + _halo_specs(tm, CONV_CH, n_tok)
        + [tok_spec(2 * D_MODEL), tok_spec(D_MODEL),
           pl.BlockSpec((None, 1, N_MOD * D_MODEL), mod_idx),
           _const_spec((ATTN_W, D_MODEL)), _const_spec((DW_WIDTH, CONV_CH)),
           _const_spec((1, CONV_CH)), _const_spec((1, CONV_CH)),
           _const_spec((CONV_CH, D_MODEL)), _const_spec((D_MODEL, D_MODEL)),
           _const_spec((1, D_MODEL))])
    return pl.pallas_call(
        functools.partial(_merge_kernel, tm=tm, tps=tps),
        grid=(n_tok // tm,),
        in_specs=in_specs,
        out_specs=[tok_spec(D_MODEL), tok_spec(D_MODEL)],
        out_shape=[jax.ShapeDtypeStruct((n_tok, D_MODEL), F32),
                   jax.ShapeDtypeStruct((n_tok, D_MODEL), BF16)],
        scratch_shapes=[pltpu.VMEM((CONV_CH // LANES, tm + 2 * HALO, LANES), F32),
                        pltpu.VMEM((tm, CONV_CH), F32)],
        compiler_params=_params(1),
        name="merge",
    )(o_n, glu, glu, glu, gates, x2d, mod3, w_attn_proj, w_conv_dw, ln_g, ln_b,
      w_conv_proj, w_out, w_norm2)


def _ffn_kernel(h_ref, h_prev, h_next, x1_ref, mod_ref, wup_ref, wdw_ref, wdn_ref, wfn_ref,
                y_ref, lhs_ref, u_ref, acc_ref, *, tm, tps, cn):
    _fill_ext(lhs_ref, h_ref, h_prev, h_next, tm, tps)
    lhs = lhs_ref[...]
    n_slabs = cn // LANES

    def up_proj(par, half, c0):
        res = jnp.dot(lhs, wup_ref[:, c0:c0 + cn], preferred_element_type=F32)
        for s in range(n_slabs):
            u_ref[par, half, s] = res[:, s * LANES:(s + 1) * LANES]

    def conv(par, half, c0):
        outs = []
        for s in range(n_slabs):
            cols = slice(c0 + s * LANES, c0 + (s + 1) * LANES)
            taps = [u_ref[par, half, s, pl.ds(HALO - 1 + j, tm, stride=1), :] * wdw_ref[j:j + 1, cols]
                    for j in range(FFN_DW_WIDTH)]
            outs.append(taps[0] + taps[1] + taps[2])
        return jnp.concatenate(outs, axis=1)

    n_chunks = D_FF // cn
    up_proj(0, 0, 0)
    up_proj(0, 1, D_FF)
    for c in range(n_chunks):
        a0 = c * cn
        b0 = D_FF + c * cn
        par = c % 2
        if c + 1 < n_chunks:
            up_proj(1 - par, 0, a0 + cn)
            up_proj(1 - par, 1, b0 + cn)
        a = conv(par, 0, a0)
        b = conv(par, 1, b0)
        act = (a * _sigmoid(a) * b).astype(BF16)
        down = jnp.dot(act, wdn_ref[a0:a0 + cn, :], preferred_element_type=F32)
        if c == 0:
            acc_ref[...] = down
        else:
            acc_ref[...] += down

    gate2 = mod_ref[:, 5 * D_MODEL:6 * D_MODEL]
    y = x1_ref[...] + gate2 * acc_ref[...]
    ms = jnp.mean(y * y, axis=-1, keepdims=True)
    y_ref[...] = y * lax.rsqrt(ms + EPS) * wfn_ref[...]


def _ffn_call(h2, x1, mod3, w_up, w_ffn_dw, w_down, w_final_norm, seq, tm, per_seq_mod):
    n_tok = x1.shape[0]
    tps = seq // tm
    cn = 256
    mod_idx = (lambda i: (i // tps, 0, 0)) if per_seq_mod else (lambda i: (0, 0, 0))
    tok_spec = lambda w: pl.BlockSpec((tm, w), lambda i: (i, 0))
    in_specs = (
        _halo_specs(tm, D_MODEL, n_tok)
        + [tok_spec(D_MODEL), pl.BlockSpec((None, 1, N_MOD * D_MODEL), mod_idx),
           _const_spec((D_MODEL, 2 * D_FF)), _const_spec((FFN_DW_WIDTH, 2 * D_FF)),
           _const_spec((D_FF, D_MODEL)), _const_spec((1, D_MODEL))])
    return pl.pallas_call(
        functools.partial(_ffn_kernel, tm=tm, tps=tps, cn=cn),
        grid=(n_tok // tm,),
        in_specs=in_specs,
        out_specs=tok_spec(D_MODEL),
        out_shape=jax.ShapeDtypeStruct((n_tok, D_MODEL), F32),
        scratch_shapes=[
            pltpu.VMEM((tm + 2 * HALO, D_MODEL), BF16),
            pltpu.VMEM((2, 2, cn // LANES, tm + 2 * HALO, LANES), F32),
            pltpu.VMEM((tm, D_MODEL), F32),
        ],
        compiler_params=_params(1),
        name="ffn",
    )(h2, h2, h2, x1, mod3, w_up, w_ffn_dw, w_down, w_final_norm)


def _tiles(seq):
    tm = min(seq, 512)
    tq = min(seq, 512)
    tk = min(seq, 2048)
    heads = N_HEADS
    return tm, tq, tk, heads


def _trunk_group(x, mod3, per_seq_mod, use_rope, cache, lp):
    n_seq, seq, _ = x.shape
    tm, tq, tk, heads = _tiles(seq)
    x2d = x.reshape(n_seq * seq, D_MODEL)
    emit_f32_kv = cache is None
    outs = _in_proj_call(x2d, mod3, lp["w_norm1"], lp["w_in"], seq, tm,
                         per_seq_mod, use_rope, emit_f32_kv)
    q, k, v, glu, gates = outs[:5]
    cache_k = cache_v = None
    if cache is not None:
        cache_k, cache_v = cache
    o_n = _attn_call(q, k, v, cache_k, cache_v, lp["lam"], lp["w_head_norm"],
                     n_seq, seq, tq, tk, heads)
    x1, h2 = _merge_call(o_n, glu, gates, x2d, mod3, lp["w_attn_proj"], lp["w_conv_dw"],
                         lp["conv_ln_g"], lp["conv_ln_b"], lp["w_conv_proj"], lp["w_out"],
                         lp["w_norm2"], seq, tm, per_seq_mod)
    y = _ffn_call(h2, x1, mod3, lp["w_up"], lp["w_ffn_dw"], lp["w_down"],
                  lp["w_final_norm"], seq, tm, per_seq_mod)
    return y.reshape(n_seq, seq, D_MODEL), outs[5:]


def kernel(x_prompt, x_sample, cache_k, cache_v, c, c_ctx, w_ada, b_ada, w_norm1, w_in, lambda_q1, lambda_k1, lambda_q2, lambda_k2, w_head_norm, w_attn_proj, w_conv_dw, conv_ln_g, conv_ln_b, w_conv_proj, w_out, w_norm2, w_up, w_ffn_dw, w_down, w_final_norm):
    assert w_in.shape[0] == 1, "single trunk layer"
    n_dec = x_sample.shape[0]
    n_ctx, seq_ctx = x_prompt.shape[0], x_prompt.shape[1]

    mod_rows = 16
    cc = jnp.concatenate(
        [c, c_ctx[None, :], jnp.zeros((mod_rows - n_dec - 1, D_MODEL), F32)], axis=0)
    mod = _mod_call(cc, w_ada[0], b_ada)
    mod_lat = mod[:n_dec].reshape(n_dec, 1, N_MOD * D_MODEL)
    mod_ctx = mod[n_dec:n_dec + 1].reshape(1, 1, N_MOD * D_MODEL)

    lp = dict(
        w_norm1=w_norm1, w_in=w_in[0].astype(BF16),
        lam=(lambda_q1, lambda_k1, lambda_q2, lambda_k2), w_head_norm=w_head_norm,
        w_attn_proj=w_attn_proj[0].astype(BF16), w_conv_dw=w_conv_dw[0],
        conv_ln_g=conv_ln_g, conv_ln_b=conv_ln_b,
        w_conv_proj=w_conv_proj[0].astype(BF16), w_out=w_out[0].astype(BF16),
        w_norm2=w_norm2, w_up=w_up[0].astype(BF16), w_ffn_dw=w_ffn_dw[0],
        w_down=w_down[0].astype(BF16), w_final_norm=w_final_norm[None, :])

    y_prompt, (kf, vf) = _trunk_group(x_prompt, mod_ctx, False, False, None, lp)
    cache = (cache_k[:, 0], cache_v[:, 0])
    y_sample, _ = _trunk_group(x_sample, mod_lat, True, True, cache, lp)
    new_k = kf.reshape(n_ctx, 1, seq_ctx, N_HEADS, HEAD_W)
    new_v = vf.reshape(n_ctx, 1, seq_ctx, N_HEADS, V_DIM)
    return (y_prompt, y_sample, new_k, new_v)
```

```python
import functools

import jax
import jax.numpy as jnp
import numpy as np
from jax import lax
from jax.experimental import pallas as pl
from jax.experimental.pallas import tpu as pltpu

D_MODEL = 1024
N_HEADS = 4
QK_DIM = 64
V_DIM = 2 * QK_DIM
HEAD_W = 2 * QK_DIM
ATTN_W = N_HEADS * V_DIM
CONV_CH = D_MODEL // 2
DW_WIDTH = 31
D_FF = 2816
FFN_DW_WIDTH = 3
GRID_W = 64
ROPE_THETA = 10000.0
EPS = 1e-6
N_MOD = 6
LAM_INIT = 0.8 - 0.6 * float(np.exp(-0.3 * 0))
QK_SCALE = QK_DIM ** -0.5 * float(np.log2(np.e))

MAX_EXCESS = 16.0
LANES = 128
HALO = 16
CONV_PAD = (DW_WIDTH - 1) // 2
CONV_ROWS = 128
VMEM_LIMIT = 56 * 1024 * 1024

F32 = jnp.float32
BF16 = jnp.bfloat16


def _sigmoid(x):
    return 1.0 / (1.0 + jnp.exp(-x))


def _const_spec(shape):
    nd = len(shape)
    return pl.BlockSpec(shape, lambda *_: (0,) * nd, pipeline_mode=pl.Buffered(1))


def _params(n_axes):
    return pltpu.CompilerParams(
        dimension_semantics=("arbitrary",) * n_axes, vmem_limit_bytes=VMEM_LIMIT)


def _mod_kernel(c_ref, w_ref, b_ref, o_ref):
    c = c_ref[...]
    s = c * _sigmoid(c)
    w = w_ref[...]
    s_hi = s.astype(BF16)
    s_lo = (s - s_hi.astype(F32)).astype(BF16)
    w_hi = w.astype(BF16)
    w_lo = (w - w_hi.astype(F32)).astype(BF16)
    acc = jnp.dot(s_hi, w_hi, preferred_element_type=F32)
    acc += jnp.dot(s_hi, w_lo, preferred_element_type=F32)
    acc += jnp.dot(s_lo, w_hi, preferred_element_type=F32)
    o_ref[...] = acc + b_ref[...]


def _mod_call(cc, w_ada, b_ada):
    rows = cc.shape[0]
    n_out = w_ada.shape[1]
    bn = 1536
    return pl.pallas_call(
        _mod_kernel,
        grid=(n_out // bn,),
        in_specs=[
            pl.BlockSpec((rows, D_MODEL), lambda j: (0, 0)),
            pl.BlockSpec((D_MODEL, bn), lambda j: (0, j)),
            pl.BlockSpec((1, bn), lambda j: (0, j)),
        ],
        out_specs=pl.BlockSpec((rows, bn), lambda j: (0, j)),
        out_shape=jax.ShapeDtypeStruct((rows, n_out), F32),
        compiler_params=_params(1),
        name="mod",
    )(cc, w_ada, b_ada)


def _rope_tables(seq):
    t = np.arange(seq)
    row = (t // GRID_W).astype(np.float32).astype(np.float64)
    col = (t % GRID_W).astype(np.float32).astype(np.float64)
    half = QK_DIM // 2
    freqs = ROPE_THETA ** (-np.arange(0, half, 2, dtype=np.float64) / half)
    ar = row[:, None] * freqs
    ac = col[:, None] * freqs
    cos = np.concatenate([np.cos(ar), np.cos(ar), np.cos(ac), np.cos(ac)], axis=1)
    sin = np.concatenate([-np.sin(ar), np.sin(ar), -np.sin(ac), np.sin(ac)], axis=1)
    cos = np.tile(cos, (1, HEAD_W // QK_DIM)).astype(np.float32)
    sin = np.tile(sin, (1, HEAD_W // QK_DIM)).astype(np.float32)
    return jnp.asarray(cos), jnp.asarray(sin)


def _rope(x, cos, sin):
    quarter = QK_DIM // 4
    lane = lax.broadcasted_iota(jnp.int32, (1, HEAD_W), 1)
    fwd = pltpu.roll(x, HEAD_W - quarter, 1)
    bwd = pltpu.roll(x, quarter, 1)
    partner = jnp.where((lane & quarter) == 0, fwd, bwd)
    return x * cos + partner * sin


def _in_proj_kernel(*refs, use_rope, emit_f32_kv):
    it = iter(refs)
    x_ref, mod_ref, wn_ref, w_ref = next(it), next(it), next(it), next(it)
    cos_ref = sin_ref = None
    if use_rope:
        cos_ref, sin_ref = next(it), next(it)
    qt_ref, k_ref, vt_ref, glu_ref, gate_ref = next(it), next(it), next(it), next(it), next(it)
    kf_ref = vf_ref = None
    if emit_f32_kv:
        kf_ref, vf_ref = next(it), next(it)

    x = x_ref[...]
    shift = mod_ref[:, 0:D_MODEL]
    scale = mod_ref[:, D_MODEL:2 * D_MODEL]
    ms = jnp.mean(x * x, axis=-1, keepdims=True)
    h = x * lax.rsqrt(ms + EPS) * wn_ref[...]
    h = h * (1.0 + scale) + shift
    hb = h.astype(BF16)

    def proj(c0, c1):
        return jnp.dot(hb, w_ref[:, c0:c1], preferred_element_type=F32)

    pq = proj(0, ATTN_W)
    pk = proj(ATTN_W, 2 * ATTN_W)
    if emit_f32_kv:
        kf_ref[...] = pk
    for hd in range(N_HEADS):
        sl = slice(hd * HEAD_W, (hd + 1) * HEAD_W)
        qh = pq[:, sl]
        kh = pk[:, sl]
        if use_rope:
            cos = cos_ref[...]
            sin = sin_ref[...]
            qh = _rope(qh, cos, sin)
            kh = _rope(kh, cos, sin)
        qt_ref[sl, :] = (qh * QK_SCALE).T.astype(BF16)
        k_ref[:, sl] = kh.astype(BF16)

    pv = proj(2 * ATTN_W, 3 * ATTN_W)
    if emit_f32_kv:
        vf_ref[...] = pv
    for hd in range(N_HEADS):
        sl = slice(hd * HEAD_W, (hd + 1) * HEAD_W)
        vt_ref[sl, :] = pv[:, sl].T.astype(BF16)

    u0 = 3 * ATTN_W
    pu = proj(u0, u0 + 2 * CONV_CH)
    glu_ref[...] = (pu[:, :CONV_CH] * _sigmoid(pu[:, CONV_CH:])).astype(BF16)

    g0 = u0 + 2 * CONV_CH
    for j in range(2):
        pg = proj(g0 + j * D_MODEL, g0 + (j + 1) * D_MODEL)
        gate_ref[:, j * D_MODEL:(j + 1) * D_MODEL] = _sigmoid(pg).astype(BF16)


def _in_proj_call(x2d, mod3, w_norm1, w_in_bf, seq, tm, per_seq_mod, use_rope, emit_f32_kv):
    n_tok = x2d.shape[0]
    tps = seq // tm
    in_cols = w_in_bf.shape[1]
    mod_idx = (lambda i: (i // tps, 0, 0)) if per_seq_mod else (lambda i: (0, 0, 0))
    in_specs = [
        pl.BlockSpec((tm, D_MODEL), lambda i: (i, 0)),
        pl.BlockSpec((None, 1, N_MOD * D_MODEL), mod_idx),
        _const_spec((1, D_MODEL)),
        _const_spec((D_MODEL, in_cols)),
    ]
    args = [x2d, mod3, w_norm1, w_in_bf]
    if use_rope:
        cos, sin = _rope_tables(seq)
        in_specs += [pl.BlockSpec((tm, HEAD_W), lambda i: (i % tps, 0))] * 2
        args += [cos, sin]
    tok_spec = lambda w: pl.BlockSpec((tm, w), lambda i: (i, 0))
    tr_spec = pl.BlockSpec((ATTN_W, tm), lambda i: (0, i))
    tr_shape = jax.ShapeDtypeStruct((ATTN_W, n_tok), BF16)
    out_specs = [tr_spec, tok_spec(ATTN_W), tr_spec, tok_spec(CONV_CH), tok_spec(2 * D_MODEL)]
    out_shape = [tr_shape, jax.ShapeDtypeStruct((n_tok, ATTN_W), BF16), tr_shape,
                 jax.ShapeDtypeStruct((n_tok, CONV_CH), BF16),
                 jax.ShapeDtypeStruct((n_tok, 2 * D_MODEL), BF16)]
    if emit_f32_kv:
        out_specs += [tok_spec(ATTN_W)] * 2
        out_shape += [jax.ShapeDtypeStruct((n_tok, ATTN_W), F32)] * 2
    return pl.pallas_call(
        functools.partial(_in_proj_kernel, use_rope=use_rope, emit_f32_kv=emit_f32_kv),
        grid=(n_tok // tm,),
        in_specs=in_specs,
        out_specs=out_specs,
        out_shape=out_shape,
        compiler_params=_params(1),
        name="in_proj",
    )(*args)


def _attn_kernel(*refs, tq, tk, n_chunks, heads, use_cache):
    it = iter(refs)
    qt_ref, k_ref, vt_ref = next(it), next(it), next(it)
    ck_ref = cv_ref = None
    if use_cache:
        ck_ref, cv_ref = next(it), next(it)
    lq1, lk1, lq2, lk2, whn_ref = next(it), next(it), next(it), next(it), next(it)
    o_ref = next(it)
    for hd in range(heads):
        _attend_head(slice(hd * HEAD_W, (hd + 1) * HEAD_W), qt_ref, k_ref, vt_ref, ck_ref, cv_ref,
                     (lq1, lk1, lq2, lk2), whn_ref, o_ref, tq, tk, n_chunks)


def _attend_head(hs, qt_ref, k_ref, vt_ref, ck_ref, cv_ref, lam_refs, whn_ref, o_ref, tq, tk, n_chunks):
    lq1, lk1, lq2, lk2 = lam_refs
    qt = qt_ref[hs, :]
    zero = jnp.zeros((QK_DIM, tq), BF16)
    qs = jnp.concatenate(
        [jnp.concatenate([qt[0:QK_DIM, :], zero], axis=0),
         jnp.concatenate([zero, qt[QK_DIM:HEAD_W, :]], axis=0)], axis=1)

    chunks = [(k_ref[j * tk:(j + 1) * tk, hs], vt_ref[hs, j * tk:(j + 1) * tk])
              for j in range(n_chunks)]
    if ck_ref is not None:
        chunks.insert(0, (ck_ref[0, :, hs], cv_ref[0, :, hs].astype(F32).T.astype(BF16)))

    def finalize(l, acc):
        ot = acc / l
        lam = (jnp.exp(jnp.sum(lq1[...] * lk1[...], axis=-1, keepdims=True))
               - jnp.exp(jnp.sum(lq2[...] * lk2[...], axis=-1, keepdims=True)) + LAM_INIT)
        o = (ot[:, 0:tq] - lam * ot[:, tq:2 * tq]).T
        ms = jnp.mean(o * o, axis=-1, keepdims=True)
        o = o * lax.rsqrt(ms + EPS) * whn_ref[...] * (1.0 - LAM_INIT)
        o_ref[:, hs] = o.astype(BF16)

    kb, vtb = chunks[0]
    s = jnp.dot(kb, qs, preferred_element_type=F32)
    ref_row = jnp.max(s, axis=0, keepdims=True).astype(BF16).astype(F32)
    p = jnp.exp2(s - ref_row)
    l = jnp.sum(p, axis=0, keepdims=True)
    acc = jnp.dot(vtb, p.astype(BF16), preferred_element_type=F32)
    if len(chunks) == 1:
        finalize(l, acc)
        return

    row = lax.broadcasted_iota(jnp.int32, (16, 2 * tq), 0)
    neg_ref = jnp.where(row == 0, -ref_row, 0.0).astype(BF16)
    qs_aug = jnp.concatenate(
        [qs, neg_ref, jnp.zeros((HEAD_W - 16, 2 * tq), BF16)], axis=0)
    excess = jnp.zeros((1, 2 * tq), F32)
    for kb, vtb in chunks[1:]:
        k_aug = jnp.concatenate([kb, jnp.ones((kb.shape[0], HEAD_W), BF16)], axis=1)
        s = jnp.dot(k_aug, qs_aug, preferred_element_type=F32)
        p = jnp.exp2(s)
        excess = jnp.maximum(excess, jnp.max(s, axis=0, keepdims=True))
        l = l + jnp.sum(p, axis=0, keepdims=True)
        acc = acc + jnp.dot(vtb, p.astype(BF16), preferred_element_type=F32)
    in_range = jnp.max(excess) <= MAX_EXCESS

    @pl.when(in_range)
    def _():
        finalize(l, acc)

    @pl.when(jnp.logical_not(in_range))
    def _():
        def update(state, kb, vtb):
            m_prev, l_prev, acc = state
            s = jnp.dot(kb, qs, preferred_element_type=F32)
            m_new = jnp.maximum(m_prev, jnp.max(s, axis=0, keepdims=True))
            alpha = jnp.exp2(m_prev - m_new)
            p = jnp.exp2(s - m_new)
            l_new = alpha * l_prev + jnp.sum(p, axis=0, keepdims=True)
            acc = alpha * acc + jnp.dot(vtb, p.astype(BF16), preferred_element_type=F32)
            return m_new, l_new, acc

        state = (jnp.full((1, 2 * tq), -jnp.inf, F32), jnp.zeros((1, 2 * tq), F32),
                 jnp.zeros((V_DIM, 2 * tq), F32))
        for kb, vtb in chunks:
            state = update(state, kb, vtb)
        finalize(state[1], state[2])


def _attn_call(qt, k, vt, cache_k, cache_v, lam_params, w_head_norm, n_seq, seq, tq, tk, heads):
    n_tok = k.shape[0]
    qps = seq // tq
    use_cache = cache_k is not None
    hw = heads * HEAD_W
    in_specs = [
        pl.BlockSpec((hw, tq), lambda b, h, i: (h, b * qps + i)),
        pl.BlockSpec((seq, hw), lambda b, h, i: (b, h)),
        pl.BlockSpec((hw, seq), lambda b, h, i: (h, b)),
    ]
    args = [qt, k, vt]
    if use_cache:
        past = cache_k.shape[1]
        in_specs += [pl.BlockSpec((1, past, hw), lambda b, h, i: (b, 0, h))] * 2
        args += [cache_k, cache_v]
    in_specs += [_const_spec((1, QK_DIM))] * 4 + [_const_spec((1, V_DIM))]
    args += list(lam_params) + [w_head_norm]
    return pl.pallas_call(
        functools.partial(_attn_kernel, tq=tq, tk=tk, n_chunks=seq // tk, heads=heads,
                          use_cache=use_cache),
        grid=(n_seq, N_HEADS // heads, qps),
        in_specs=in_specs,
        out_specs=pl.BlockSpec((tq, hw), lambda b, h, i: (b * qps + i, h)),
        out_shape=jax.ShapeDtypeStruct((n_tok, ATTN_W), BF16),
        compiler_params=_params(3),
        name="attn",
    )(*args)


def _halo_specs(tm, width, n_tok):
    r = tm // HALO
    last = n_tok // HALO - 1
    return [
        pl.BlockSpec((tm, width), lambda i: (i, 0)),
        pl.BlockSpec((HALO, width), lambda i: (jnp.maximum(i * r - 1, 0), 0)),
        pl.BlockSpec((HALO, width), lambda i: (jnp.minimum((i + 1) * r, last), 0)),
    ]


def _fill_ext(ext_ref, main_ref, prev_ref, next_ref, tm, tps):
    j = pl.program_id(0) % tps
    prev = prev_ref[...].astype(ext_ref.dtype)
    nxt = next_ref[...].astype(ext_ref.dtype)
    ext_ref[0:HALO, :] = jnp.where(j > 0, prev, jnp.zeros_like(prev))
    ext_ref[HALO:HALO + tm, :] = main_ref[...].astype(ext_ref.dtype)
    ext_ref[HALO + tm:, :] = jnp.where(j < tps - 1, nxt, jnp.zeros_like(nxt))


def _merge_kernel(o_ref, glu_ref, glu_prev, glu_next, gate_ref, x_ref, mod_ref,
                  wap_ref, wdw_ref, lng_ref, lnb_ref, wcp_ref, wout_ref, wn2_ref,
                  x1_ref, h2_ref, ext_ref, conv_ref, *, tm, tps):
    j = pl.program_id(0) % tps
    for s in range(CONV_CH // LANES):
        cols = slice(s * LANES, (s + 1) * LANES)
        prev = glu_prev[:, cols].astype(F32)
        nxt = glu_next[:, cols].astype(F32)
        ext_ref[s, 0:HALO, :] = jnp.where(j > 0, prev, jnp.zeros_like(prev))
        ext_ref[s, HALO:HALO + tm, :] = glu_ref[:, cols].astype(F32)
        ext_ref[s, HALO + tm:, :] = jnp.where(j < tps - 1, nxt, jnp.zeros_like(nxt))
    for s in range(CONV_CH // LANES):
        cols = slice(s * LANES, (s + 1) * LANES)
        for r0 in range(0, tm, CONV_ROWS):
            acc = None
            for t in range(DW_WIDTH):
                tap = (ext_ref[s, pl.ds(HALO - CONV_PAD + r0 + t, CONV_ROWS, stride=1), :]
                       * wdw_ref[t:t + 1, cols])
                acc = tap if acc is None else acc + tap
            conv_ref[r0:r0 + CONV_ROWS, cols] = acc
    acc = conv_ref[...]
    mu = jnp.mean(acc, axis=-1, keepdims=True)
    d = acc - mu
    var = jnp.mean(d * d, axis=-1, keepdims=True)
    y = d * lax.rsqrt(var + EPS) * lng_ref[...] + lnb_ref[...]
    cv = (y * _sigmoid(y)).astype(BF16)
    conv_out = jnp.dot(cv, wcp_ref[...], preferred_element_type=F32)
    attn_out = jnp.dot(o_ref[...], wap_ref[...], preferred_element_type=F32)
    merged = (gate_ref[:, 0:D_MODEL].astype(F32) * attn_out
              + gate_ref[:, D_MODEL:2 * D_MODEL].astype(F32) * conv_out)
    mix = jnp.dot(merged.astype(BF16), wout_ref[...], preferred_element_type=F32)
    gate1 = mod_ref[:, 2 * D_MODEL:3 * D_MODEL]
    shift2 = mod_ref[:, 3 * D_MODEL:4 * D_MODEL]
    scale2 = mod_ref[:, 4 * D_MODEL:5 * D_MODEL]
    x1 = x_ref[...] + gate1 * mix
    x1_ref[...] = x1
    ms = jnp.mean(x1 * x1, axis=-1, keepdims=True)
    h2 = x1 * lax.rsqrt(ms + EPS) * wn2_ref[...]
    h2_ref[...] = (h2 * (1.0 + scale2) + shift2).astype(BF16)


def _merge_call(o_n, glu, gates, x2d, mod3, w_attn_proj, w_conv_dw, ln_g, ln_b,
                w_conv_proj, w_out, w_norm2, seq, tm, per_seq_mod):
    n_tok = x2d.shape[0]
    tps = seq // tm
    mod_idx = (lambda i: (i // tps, 0, 0)) if per_seq_mod else (lambda i: (0, 0, 0))
    tok_spec = lambda w: pl.BlockSpec((tm, w), lambda i: (i, 0))
    in_specs = (
        [tok_spec(ATTN_W)] + _halo_specs(tm, CONV_CH, n_tok)
        + [tok_spec(2 * D_MODEL), tok_spec(D_MODEL),
           pl.BlockSpec((None, 1, N_MOD * D_MODEL), mod_idx),
           _const_spec((ATTN_W, D_MODEL)), _const_spec((DW_WIDTH, CONV_CH)),
           _const_spec((1, CONV_CH)), _const_spec((1, CONV_CH)),
           _const_spec((CONV_CH, D_MODEL)), _const_spec((D_MODEL, D_MODEL)),
           _const_spec((1, D_MODEL))])
    return pl.pallas_call(
        functools.partial(_merge_kernel, tm=tm, tps=tps),
        grid=(n_tok // tm,),
        in_specs=in_specs,
        out_specs=[tok_spec(D_MODEL), tok_spec(D_MODEL)],
        out_shape=[jax.ShapeDtypeStruct((n_tok, D_MODEL), F32),
                   jax.ShapeDtypeStruct((n_tok, D_MODEL), BF16)],
        scratch_shapes=[pltpu.VMEM((CONV_CH // LANES, tm + 2 * HALO, LANES), F32),
                        pltpu.VMEM((tm, CONV_CH), F32)],
        compiler_params=_params(1),
        name="merge",
    )(o_n, glu, glu, glu, gates, x2d, mod3, w_attn_proj, w_conv_dw, ln_g, ln_b,
      w_conv_proj, w_out, w_norm2)


def _ffn_kernel(h_ref, h_prev, h_next, x1_ref, mod_ref, wup_ref, wdw_ref, wdn_ref, wfn_ref,
                y_ref, lhs_ref, u_ref, acc_ref, *, tm, tps, cn):
    _fill_ext(lhs_ref, h_ref, h_prev, h_next, tm, tps)
    lhs = lhs_ref[...]
    n_slabs = cn // LANES

    def up_proj(par, half, c0):
        res = jnp.dot(lhs, wup_ref[:, c0:c0 + cn], preferred_element_type=F32)
        for s in range(n_slabs):
            u_ref[par, half, s] = res[:, s * LANES:(s + 1) * LANES]

    def conv(par, half, c0):
        outs = []
        for s in range(n_slabs):
            cols = slice(c0 + s * LANES, c0 + (s + 1) * LANES)
            taps = [u_ref[par, half, s, pl.ds(HALO - 1 + j, tm, stride=1), :] * wdw_ref[j:j + 1, cols]
                    for j in range(FFN_DW_WIDTH)]
            outs.append(taps[0] + taps[1] + taps[2])
        return jnp.concatenate(outs, axis=1)

    n_chunks = D_FF // cn
    up_proj(0, 0, 0)
    up_proj(0, 1, D_FF)
    for c in range(n_chunks):
        a0 = c * cn
        b0 = D_FF + c * cn
        par = c % 2
        if c + 1 < n_chunks:
            up_proj(1 - par, 0, a0 + cn)
            up_proj(1 - par, 1, b0 + cn)
        a = conv(par, 0, a0)
        b = conv(par, 1, b0)
        act = (a * _sigmoid(a) * b).astype(BF16)
        down = jnp.dot(act, wdn_ref[a0:a0 + cn, :], preferred_element_type=F32)
        if c == 0:
            acc_ref[...] = down
        else:
            acc_ref[...] += down

    gate2 = mod_ref[:, 5 * D_MODEL:6 * D_MODEL]
    y = x1_ref[...] + gate2 * acc_ref[...]
    ms = jnp.mean(y * y, axis=-1, keepdims=True)
    y_ref[...] = y * lax.rsqrt(ms + EPS) * wfn_ref[...]


def _ffn_call(h2, x1, mod3, w_up, w_ffn_dw, w_down, w_final_norm, seq, tm, per_seq_mod):
    n_tok = x1.shape[0]
    tps = seq // tm
    cn = 256
    mod_idx = (lambda i: (i // tps, 0, 0)) if per_seq_mod else (lambda i: (0, 0, 0))
    tok_spec = lambda w: pl.BlockSpec((tm, w), lambda i: (i, 0))
    in_specs = (
        _halo_specs(tm, D_MODEL, n_tok)
        + [tok_spec(D_MODEL), pl.BlockSpec((None, 1, N_MOD * D_MODEL), mod_idx),
           _const_spec((D_MODEL, 2 * D_FF)), _const_spec((FFN_DW_WIDTH, 2 * D_FF)),
           _const_spec((D_FF, D_MODEL)), _const_spec((1, D_MODEL))])
    return pl.pallas_call(
        functools.partial(_ffn_kernel, tm=tm, tps=tps, cn=cn),
        grid=(n_tok // tm,),
        in_specs=in_specs,
        out_specs=tok_spec(D_MODEL),
        out_shape=jax.ShapeDtypeStruct((n_tok, D_MODEL), F32),
        scratch_shapes=[
            pltpu.VMEM((tm + 2 * HALO, D_MODEL), BF16),
            pltpu.VMEM((2, 2, cn // LANES, tm + 2 * HALO, LANES), F32),
            pltpu.VMEM((tm, D_MODEL), F32),
        ],
        compiler_params=_params(1),
        name="ffn",
    )(h2, h2, h2, x1, mod3, w_up, w_ffn_dw, w_down, w_final_norm)


def _tiles(seq):
    tm = min(seq, 512)
    tq = min(seq, 1024)
    tk = min(seq, 2048)
    heads = N_HEADS if seq <= 256 else 1
    return tm, tq, tk, heads


def _trunk_group(x, mod3, per_seq_mod, use_rope, cache, lp):
    n_seq, seq, _ = x.shape
    tm, tq, tk, heads = _tiles(seq)
    x2d = x.reshape(n_seq * seq, D_MODEL)
    emit_f32_kv = cache is None
    outs = _in_proj_call(x2d, mod3, lp["w_norm1"], lp["w_in"], seq, tm,
                         per_seq_mod, use_rope, emit_f32_kv)
    q, k, v, glu, gates = outs[:5]
    cache_k = cache_v = None
    if cache is not None:
        cache_k, cache_v = cache
    o_n = _attn_call(q, k, v, cache_k, cache_v, lp["lam"], lp["w_head_norm"],
                     n_seq, seq, tq, tk, heads)
    x1, h2 = _merge_call(o_n, glu, gates, x2d, mod3, lp["w_attn_proj"], lp["w_conv_dw"],
                         lp["conv_ln_g"], lp["conv_ln_b"], lp["w_conv_proj"], lp["w_out"],
                         lp["w_norm2"], seq, tm, per_seq_mod)
    y = _ffn_call(h2, x1, mod3, lp["w_up"], lp["w_ffn_dw"], lp["w_down"],
                  lp["w_final_norm"], seq, tm, per_seq_mod)
    return y.reshape(n_seq, seq, D_MODEL), outs[5:]


def kernel(x_prompt, x_sample, cache_k, cache_v, c, c_ctx, w_ada, b_ada, w_norm1, w_in, lambda_q1, lambda_k1, lambda_q2, lambda_k2, w_head_norm, w_attn_proj, w_conv_dw, conv_ln_g, conv_ln_b, w_conv_proj, w_out, w_norm2, w_up, w_ffn_dw, w_down, w_final_norm):
    assert w_in.shape[0] == 1, "single trunk layer"
    n_dec = x_sample.shape[0]
    n_ctx, seq_ctx = x_prompt.shape[0], x_prompt.shape[1]
    past = cache_k.shape[2]

    mod_rows = 16
    cc = jnp.concatenate(
        [c, c_ctx[None, :], jnp.zeros((mod_rows - n_dec - 1, D_MODEL), F32)], axis=0)
    mod = _mod_call(cc, w_ada[0], b_ada)
    mod_lat = mod[:n_dec].reshape(n_dec, 1, N_MOD * D_MODEL)
    mod_ctx = mod[n_dec:n_dec + 1].reshape(1, 1, N_MOD * D_MODEL)

    lp = dict(
        w_norm1=w_norm1, w_in=w_in[0].astype(BF16),
        lam=(lambda_q1, lambda_k1, lambda_q2, lambda_k2), w_head_norm=w_head_norm,
        w_attn_proj=w_attn_proj[0].astype(BF16), w_conv_dw=w_conv_dw[0],
        conv_ln_g=conv_ln_g, conv_ln_b=conv_ln_b,
        w_conv_proj=w_conv_proj[0].astype(BF16), w_out=w_out[0].astype(BF16),
        w_norm2=w_norm2, w_up=w_up[0].astype(BF16), w_ffn_dw=w_ffn_dw[0],
        w_down=w_down[0].astype(BF16), w_final_norm=w_final_norm[None, :])

    y_prompt, (kf, vf) = _trunk_group(x_prompt, mod_ctx, False, False, None, lp)
    cache = (cache_k[:, 0].astype(BF16).reshape(n_dec, past, ATTN_W),
             cache_v[:, 0].astype(BF16).reshape(n_dec, past, ATTN_W))
    y_sample, _ = _trunk_group(x_sample, mod_lat, True, True, cache, lp)
    new_k = kf.reshape(n_ctx, 1, seq_ctx, N_HEADS, HEAD_W)
    new_v = vf.reshape(n_ctx, 1, seq_ctx, N_HEADS, V_DIM)
    return (y_prompt, y_sample, new_k, new_v)
```

```python
import functools

import jax
import jax.numpy as jnp
import numpy as np
from jax import lax
from jax.experimental import pallas as pl
from jax.experimental.pallas import tpu as pltpu

D_MODEL = 1024
N_HEADS = 4
QK_DIM = 64
V_DIM = 2 * QK_DIM
HEAD_W = 2 * QK_DIM
ATTN_W = N_HEADS * V_DIM
CONV_CH = D_MODEL // 2
DW_WIDTH = 31
D_FF = 2816
FFN_DW_WIDTH = 3
GRID_W = 64
ROPE_THETA = 10000.0
EPS = 1e-6
N_MOD = 6
LAM_INIT = 0.8 - 0.6 * float(np.exp(-0.3 * 0))
QK_SCALE = QK_DIM ** -0.5 * float(np.log2(np.e))

MAX_EXCESS = 16.0
LANES = 128
HALO = 16
CONV_PAD = (DW_WIDTH - 1) // 2
CONV_ROWS = 128
VMEM_LIMIT = 56 * 1024 * 1024

F32 = jnp.float32
BF16 = jnp.bfloat16


def _sigmoid(x):
    return 1.0 / (1.0 + jnp.exp(-x))


def _const_spec(shape):
    nd = len(shape)
    return pl.BlockSpec(shape, lambda *_: (0,) * nd, pipeline_mode=pl.Buffered(1))


def _params(n_axes):
    return pltpu.CompilerParams(
        dimension_semantics=("arbitrary",) * n_axes, vmem_limit_bytes=VMEM_LIMIT)


def _mod_kernel(c_ref, w_ref, b_ref, o_ref):
    c = c_ref[...]
    s = c * _sigmoid(c)
    w = w_ref[...]
    s_hi = s.astype(BF16)
    s_lo = (s - s_hi.astype(F32)).astype(BF16)
    w_hi = w.astype(BF16)
    w_lo = (w - w_hi.astype(F32)).astype(BF16)
    acc = jnp.dot(s_hi, w_hi, preferred_element_type=F32)
    acc += jnp.dot(s_hi, w_lo, preferred_element_type=F32)
    acc += jnp.dot(s_lo, w_hi, preferred_element_type=F32)
    o_ref[...] = acc + b_ref[...]


def _mod_call(cc, w_ada, b_ada):
    rows = cc.shape[0]
    n_out = w_ada.shape[1]
    bn = 1536
    return pl.pallas_call(
        _mod_kernel,
        grid=(n_out // bn,),
        in_specs=[
            pl.BlockSpec((rows, D_MODEL), lambda j: (0, 0)),
            pl.BlockSpec((D_MODEL, bn), lambda j: (0, j)),
            pl.BlockSpec((1, bn), lambda j: (0, j)),
        ],
        out_specs=pl.BlockSpec((rows, bn), lambda j: (0, j)),
        out_shape=jax.ShapeDtypeStruct((rows, n_out), F32),
        compiler_params=_params(1),
        name="mod",
    )(cc, w_ada, b_ada)


def _rope_tables(seq):
    t = np.arange(seq)
    row = (t // GRID_W).astype(np.float32).astype(np.float64)
    col = (t % GRID_W).astype(np.float32).astype(np.float64)
    half = QK_DIM // 2
    freqs = ROPE_THETA ** (-np.arange(0, half, 2, dtype=np.float64) / half)
    ar = row[:, None] * freqs
    ac = col[:, None] * freqs
    cos = np.concatenate([np.cos(ar), np.cos(ar), np.cos(ac), np.cos(ac)], axis=1)
    sin = np.concatenate([-np.sin(ar), np.sin(ar), -np.sin(ac), np.sin(ac)], axis=1)
    cos = np.tile(cos, (1, HEAD_W // QK_DIM)).astype(np.float32)
    sin = np.tile(sin, (1, HEAD_W // QK_DIM)).astype(np.float32)
    return jnp.asarray(cos), jnp.asarray(sin)


def _rope(x, cos, sin):
    quarter = QK_DIM // 4
    lane = lax.broadcasted_iota(jnp.int32, (1, HEAD_W), 1)
    fwd = pltpu.roll(x, HEAD_W - quarter, 1)
    bwd = pltpu.roll(x, quarter, 1)
    partner = jnp.where((lane & quarter) == 0, fwd, bwd)
    return x * cos + partner * sin


def _in_proj_kernel(*refs, use_rope, emit_f32_kv):
    it = iter(refs)
    x_ref, mod_ref, wn_ref, w_ref = next(it), next(it), next(it), next(it)
    cos_ref = sin_ref = None
    if use_rope:
        cos_ref, sin_ref = next(it), next(it)
    qt_ref, k_ref, vt_ref, glu_ref, gate_ref = next(it), next(it), next(it), next(it), next(it)
    kf_ref = vf_ref = None
    if emit_f32_kv:
        kf_ref, vf_ref = next(it), next(it)

    x = x_ref[...]
    shift = mod_ref[:, 0:D_MODEL]
    scale = mod_ref[:, D_MODEL:2 * D_MODEL]
    ms = jnp.mean(x * x, axis=-1, keepdims=True)
    h = x * lax.rsqrt(ms + EPS) * wn_ref[...]
    h = h * (1.0 + scale) + shift
    hb = h.astype(BF16)

    def proj(c0, c1):
        return jnp.dot(hb, w_ref[:, c0:c1], preferred_element_type=F32)

    pq = proj(0, ATTN_W)
    pk = proj(ATTN_W, 2 * ATTN_W)
    if emit_f32_kv:
        kf_ref[...] = pk
    for hd in range(N_HEADS):
        sl = slice(hd * HEAD_W, (hd + 1) * HEAD_W)
        qh = pq[:, sl]
        kh = pk[:, sl]
        if use_rope:
            cos = cos_ref[...]
            sin = sin_ref[...]
            qh = _rope(qh, cos, sin)
            kh = _rope(kh, cos, sin)
        qt_ref[sl, :] = (qh * QK_SCALE).T.astype(BF16)
        k_ref[:, sl] = kh.astype(BF16)

    pv = proj(2 * ATTN_W, 3 * ATTN_W)
    if emit_f32_kv:
        vf_ref[...] = pv
    for hd in range(N_HEADS):
        sl = slice(hd * HEAD_W, (hd + 1) * HEAD_W)
        vt_ref[sl, :] = pv[:, sl].T.astype(BF16)

    u0 = 3 * ATTN_W
    pu = proj(u0, u0 + 2 * CONV_CH)
    glu_ref[...] = (pu[:, :CONV_CH] * _sigmoid(pu[:, CONV_CH:])).astype(BF16)

    g0 = u0 + 2 * CONV_CH
    for j in range(2):
        pg = proj(g0 + j * D_MODEL, g0 + (j + 1) * D_MODEL)
        gate_ref[:, j * D_MODEL:(j + 1) * D_MODEL] = _sigmoid(pg).astype(BF16)


def _in_proj_call(x2d, mod3, w_norm1, w_in_bf, seq, tm, per_seq_mod, use_rope, emit_f32_kv):
    n_tok = x2d.shape[0]
    tps = seq // tm
    in_cols = w_in_bf.shape[1]
    mod_idx = (lambda i: (i // tps, 0, 0)) if per_seq_mod else (lambda i: (0, 0, 0))
    in_specs = [
        pl.BlockSpec((tm, D_MODEL), lambda i: (i, 0)),
        pl.BlockSpec((None, 1, N_MOD * D_MODEL), mod_idx),
        _const_spec((1, D_MODEL)),
        _const_spec((D_MODEL, in_cols)),
    ]
    args = [x2d, mod3, w_norm1, w_in_bf]
    if use_rope:
        cos, sin = _rope_tables(seq)
        in_specs += [pl.BlockSpec((tm, HEAD_W), lambda i: (i % tps, 0))] * 2
        args += [cos, sin]
    tok_spec = lambda w: pl.BlockSpec((tm, w), lambda i: (i, 0))
    tr_spec = pl.BlockSpec((ATTN_W, tm), lambda i: (0, i))
    tr_shape = jax.ShapeDtypeStruct((ATTN_W, n_tok), BF16)
    out_specs = [tr_spec, tok_spec(ATTN_W), tr_spec, tok_spec(CONV_CH), tok_spec(2 * D_MODEL)]
    out_shape = [tr_shape, jax.ShapeDtypeStruct((n_tok, ATTN_W), BF16), tr_shape,
                 jax.ShapeDtypeStruct((n_tok, CONV_CH), BF16),
                 jax.ShapeDtypeStruct((n_tok, 2 * D_MODEL), BF16)]
    if emit_f32_kv:
        out_specs += [tok_spec(ATTN_W)] * 2
        out_shape += [jax.ShapeDtypeStruct((n_tok, ATTN_W), F32)] * 2
    return pl.pallas_call(
        functools.partial(_in_proj_kernel, use_rope=use_rope, emit_f32_kv=emit_f32_kv),
        grid=(n_tok // tm,),
        in_specs=in_specs,
        out_specs=out_specs,
        out_shape=out_shape,
        compiler_params=_params(1),
        name="in_proj",
    )(*args)


def _attn_kernel(*refs, tq, tk, n_chunks, heads, use_cache):
    it = iter(refs)
    qt_ref, k_ref, vt_ref = next(it), next(it), next(it)
    ck_ref = cv_ref = None
    if use_cache:
        ck_ref, cv_ref = next(it), next(it)
    lq1, lk1, lq2, lk2, whn_ref = next(it), next(it), next(it), next(it), next(it)
    o_ref = next(it)
    for hd in range(heads):
        _attend_head(slice(hd * HEAD_W, (hd + 1) * HEAD_W), qt_ref, k_ref, vt_ref, ck_ref, cv_ref,
                     (lq1, lk1, lq2, lk2), whn_ref, o_ref, tq, tk, n_chunks)


def _attend_head(hs, qt_ref, k_ref, vt_ref, ck_ref, cv_ref, lam_refs, whn_ref, o_ref, tq, tk, n_chunks):
    lq1, lk1, lq2, lk2 = lam_refs
    qt = qt_ref[hs, :]
    zero = jnp.zeros((QK_DIM, tq), BF16)
    qs = jnp.concatenate(
        [jnp.concatenate([qt[0:QK_DIM, :], zero], axis=0),
         jnp.concatenate([zero, qt[QK_DIM:HEAD_W, :]], axis=0)], axis=1)

    chunks = [(k_ref[j * tk:(j + 1) * tk, hs], vt_ref[hs, j * tk:(j + 1) * tk])
              for j in range(n_chunks)]
    if ck_ref is not None:
        chunks.insert(0, (ck_ref[0, :, hs], cv_ref[0, :, hs].astype(F32).T.astype(BF16)))

    def finalize(l, acc):
        ot = acc / l
        lam = (jnp.exp(jnp.sum(lq1[...] * lk1[...], axis=-1, keepdims=True))
               - jnp.exp(jnp.sum(lq2[...] * lk2[...], axis=-1, keepdims=True)) + LAM_INIT)
        o = (ot[:, 0:tq] - lam * ot[:, tq:2 * tq]).T
        ms = jnp.mean(o * o, axis=-1, keepdims=True)
        o = o * lax.rsqrt(ms + EPS) * whn_ref[...] * (1.0 - LAM_INIT)
        o_ref[:, hs] = o.astype(BF16)

    kb, vtb = chunks[0]
    s = jnp.dot(kb, qs, preferred_element_type=F32)
    ref_row = jnp.max(s, axis=0, keepdims=True).astype(BF16).astype(F32)
    p = jnp.exp2(s - ref_row)
    l = jnp.sum(p, axis=0, keepdims=True)
    acc = jnp.dot(vtb, p.astype(BF16), preferred_element_type=F32)
    if len(chunks) == 1:
        finalize(l, acc)
        return

    row = lax.broadcasted_iota(jnp.int32, (16, 2 * tq), 0)
    neg_ref = jnp.where(row == 0, -ref_row, 0.0).astype(BF16)
    qs_aug = jnp.concatenate(
        [qs, neg_ref, jnp.zeros((HEAD_W - 16, 2 * tq), BF16)], axis=0)
    excess = jnp.zeros((1, 2 * tq), F32)
    for kb, vtb in chunks[1:]:
        k_aug = jnp.concatenate([kb, jnp.ones((kb.shape[0], HEAD_W), BF16)], axis=1)
        s = jnp.dot(k_aug, qs_aug, preferred_element_type=F32)
        p = jnp.exp2(s)
        excess = jnp.maximum(excess, jnp.max(s, axis=0, keepdims=True))
        l = l + jnp.sum(p, axis=0, keepdims=True)
        acc = acc + jnp.dot(vtb, p.astype(BF16), preferred_element_type=F32)
    in_range = jnp.max(excess) <= MAX_EXCESS

    @pl.when(in_range)
    def _():
        finalize(l, acc)

    @pl.when(jnp.logical_not(in_range))
    def _():
        def update(state, kb, vtb):
            m_prev, l_prev, acc = state
            s = jnp.dot(kb, qs, preferred_element_type=F32)
            m_new = jnp.maximum(m_prev, jnp.max(s, axis=0, keepdims=True))
            alpha = jnp.exp2(m_prev - m_new)
            p = jnp.exp2(s - m_new)
            l_new = alpha * l_prev + jnp.sum(p, axis=0, keepdims=True)
            acc = alpha * acc + jnp.dot(vtb, p.astype(BF16), preferred_element_type=F32)
            return m_new, l_new, acc

        state = (jnp.full((1, 2 * tq), -jnp.inf, F32), jnp.zeros((1, 2 * tq), F32),
                 jnp.zeros((V_DIM, 2 * tq), F32))
        for kb, vtb in chunks:
            state = update(state, kb, vtb)
        finalize(state[1], state[2])


def _attn_call(qt, k, vt, cache_k, cache_v, lam_params, w_head_norm, n_seq, seq, tq, tk, heads):
    n_tok = k.shape[0]
    qps = seq // tq
    use_cache = cache_k is not None
    hw = heads * HEAD_W
    in_specs = [
        pl.BlockSpec((hw, tq), lambda b, h, i: (h, b * qps + i)),
        pl.BlockSpec((seq, hw), lambda b, h, i: (b, h)),
        pl.BlockSpec((hw, seq), lambda b, h, i: (h, b)),
    ]
    args = [qt, k, vt]
    if use_cache:
        past = cache_k.shape[1]
        in_specs += [pl.BlockSpec((1, past, hw), lambda b, h, i: (b, 0, h))] * 2
        args += [cache_k, cache_v]
    in_specs += [_const_spec((1, QK_DIM))] * 4 + [_const_spec((1, V_DIM))]
    args += list(lam_params) + [w_head_norm]
    return pl.pallas_call(
        functools.partial(_attn_kernel, tq=tq, tk=tk, n_chunks=seq // tk, heads=heads,
                          use_cache=use_cache),
        grid=(n_seq, N_HEADS // heads, qps),
        in_specs=in_specs,
        out_specs=pl.BlockSpec((tq, hw), lambda b, h, i: (b * qps + i, h)),
        out_shape=jax.ShapeDtypeStruct((n_tok, ATTN_W), BF16),
        compiler_params=_params(3),
        name="attn",
    )(*args)


def _halo_specs(tm, width, n_tok):
    r = tm // HALO
    last = n_tok // HALO - 1
    return [
        pl.BlockSpec((tm, width), lambda i: (i, 0)),
        pl.BlockSpec((HALO, width), lambda i: (jnp.maximum(i * r - 1, 0), 0)),
        pl.BlockSpec((HALO, width), lambda i: (jnp.minimum((i + 1) * r, last), 0)),
    ]


def _fill_ext(ext_ref, main_ref, prev_ref, next_ref, tm, tps):
    j = pl.program_id(0) % tps
    prev = prev_ref[...].astype(ext_ref.dtype)
    nxt = next_ref[...].astype(ext_ref.dtype)
    ext_ref[0:HALO, :] = jnp.where(j > 0, prev, jnp.zeros_like(prev))
    ext_ref[HALO:HALO + tm, :] = main_ref[...].astype(ext_ref.dtype)
    ext_ref[HALO + tm:, :] = jnp.where(j < tps - 1, nxt, jnp.zeros_like(nxt))


def _merge_kernel(o_ref, glu_ref, glu_prev, glu_next, gate_ref, x_ref, mod_ref,
                  wap_ref, wdw_ref, lng_ref, lnb_ref, wcp_ref, wout_ref, wn2_ref,
                  x1_ref, h2_ref, ext_ref, conv_ref, *, tm, tps):
    j = pl.program_id(0) % tps
    for s in range(CONV_CH // LANES):
        cols = slice(s * LANES, (s + 1) * LANES)
        prev = glu_prev[:, cols].astype(F32)
        nxt = glu_next[:, cols].astype(F32)
        ext_ref[s, 0:HALO, :] = jnp.where(j > 0, prev, jnp.zeros_like(prev))
        ext_ref[s, HALO:HALO + tm, :] = glu_ref[:, cols].astype(F32)
        ext_ref[s, HALO + tm:, :] = jnp.where(j < tps - 1, nxt, jnp.zeros_like(nxt))
    for s in range(CONV_CH // LANES):
        cols = slice(s * LANES, (s + 1) * LANES)
        for r0 in range(0, tm, CONV_ROWS):
            acc = None
            for t in range(DW_WIDTH):
                tap = (ext_ref[s, pl.ds(HALO - CONV_PAD + r0 + t, CONV_ROWS, stride=1), :]
                       * wdw_ref[t:t + 1, cols])
                acc = tap if acc is None else acc + tap
            conv_ref[r0:r0 + CONV_ROWS, cols] = acc
    acc = conv_ref[...]
    mu = jnp.mean(acc, axis=-1, keepdims=True)
    d = acc - mu
    var = jnp.mean(d * d, axis=-1, keepdims=True)
    y = d * lax.rsqrt(var + EPS) * lng_ref[...] + lnb_ref[...]
    cv = (y * _sigmoid(y)).astype(BF16)
    conv_out = jnp.dot(cv, wcp_ref[...], preferred_element_type=F32)
    attn_out = jnp.dot(o_ref[...], wap_ref[...], preferred_element_type=F32)
    merged = (gate_ref[:, 0:D_MODEL].astype(F32) * attn_out
              + gate_ref[:, D_MODEL:2 * D_MODEL].astype(F32) * conv_out)
    mix = jnp.dot(merged.astype(BF16), wout_ref[...], preferred_element_type=F32)
    gate1 = mod_ref[:, 2 * D_MODEL:3 * D_MODEL]
    shift2 = mod_ref[:, 3 * D_MODEL:4 * D_MODEL]
    scale2 = mod_ref[:, 4 * D_MODEL:5 * D_MODEL]
    x1 = x_ref[...] + gate1 * mix
    x1_ref[...] = x1
    ms = jnp.mean(x1 * x1, axis=-1, keepdims=True)
    h2 = x1 * lax.rsqrt(ms + EPS) * wn2_ref[...]
    h2_ref[...] = (h2 * (1.0 + scale2) + shift2).astype(BF16)


def _merge_call(o_n, glu, gates, x2d, mod3, w_attn_proj, w_conv_dw, ln_g, ln_b,
                w_conv_proj, w_out, w_norm2, seq, tm, per_seq_mod):
    n_tok = x2d.shape[0]
    tps = seq // tm
    mod_idx = (lambda i: (i // tps, 0, 0)) if per_seq_mod else (lambda i: (0, 0, 0))
    tok_spec = lambda w: pl.BlockSpec((tm, w), lambda i: (i, 0))
    in_specs = (
        [tok_spec(ATTN_W)] + _halo_specs(tm, CONV_CH, n_tok)
        + [tok_spec(2 * D_MODEL), tok_spec(D_MODEL),
           pl.BlockSpec((None, 1, N_MOD * D_MODEL), mod_idx),
           _const_spec((ATTN_W, D_MODEL)), _const_spec((DW_WIDTH, CONV_CH)),
           _const_spec((1, CONV_CH)), _const_spec((1, CONV_CH)),
           _const_spec((CONV_CH, D_MODEL)), _const_spec((D_MODEL, D_MODEL)),
           _const_spec((1, D_MODEL))])
    return pl.pallas_call(
        functools.partial(_merge_kernel, tm=tm, tps=tps),
        grid=(n_tok // tm,),
        in_specs=in_specs,
        out_specs=[tok_spec(D_MODEL), tok_spec(D_MODEL)],
        out_shape=[jax.ShapeDtypeStruct((n_tok, D_MODEL), F32),
                   jax.ShapeDtypeStruct((n_tok, D_MODEL), BF16)],
        scratch_shapes=[pltpu.VMEM((CONV_CH // LANES, tm + 2 * HALO, LANES), F32),
                        pltpu.VMEM((tm, CONV_CH), F32)],
        compiler_params=_params(1),
        name="merge",
    )(o_n, glu, glu, glu, gates, x2d, mod3, w_attn_proj, w_conv_dw, ln_g, ln_b,
      w_conv_proj, w_out, w_norm2)


def _ffn_kernel(h_ref, h_prev, h_next, x1_ref, mod_ref, wup_ref, wdw_ref, wdn_ref, wfn_ref,
                y_ref, lhs_ref, u_ref, acc_ref, *, tm, tps, cn):
    _fill_ext(lhs_ref, h_ref, h_prev, h_next, tm, tps)
    lhs = lhs_ref[...]
    n_slabs = cn // LANES

    def up_proj(par, half, c0):
        res = jnp.dot(lhs, wup_ref[:, c0:c0 + cn], preferred_element_type=F32)
        for s in range(n_slabs):
            u_ref[par, half, s] = res[:, s * LANES:(s + 1) * LANES]

    def conv(par, half, c0):
        outs = []
        for s in range(n_slabs):
            cols = slice(c0 + s * LANES, c0 + (s + 1) * LANES)
            taps = [u_ref[par, half, s, pl.ds(HALO - 1 + j, tm, stride=1), :] * wdw_ref[j:j + 1, cols]
                    for j in range(FFN_DW_WIDTH)]
            outs.append(taps[0] + taps[1] + taps[2])
        return jnp.concatenate(outs, axis=1)

    n_chunks = D_FF // cn
    up_proj(0, 0, 0)
    up_proj(0, 1, D_FF)
    for c in range(n_chunks):
        a0 = c * cn
        b0 = D_FF + c * cn
        par = c % 2
        if c + 1 < n_chunks:
            up_proj(1 - par, 0, a0 + cn)
            up_proj(1 - par, 1, b0 + cn)
        a = conv(par, 0, a0)
        b = conv(par, 1, b0)
        act = (a * _sigmoid(a) * b).astype(BF16)
        down = jnp.dot(act, wdn_ref[a0:a0 + cn, :], preferred_element_type=F32)
        if c == 0:
            acc_ref[...] = down
        else:
            acc_ref[...] += down

    gate2 = mod_ref[:, 5 * D_MODEL:6 * D_MODEL]
    y = x1_ref[...] + gate2 * acc_ref[...]
    ms = jnp.mean(y * y, axis=-1, keepdims=True)
    y_ref[...] = y * lax.rsqrt(ms + EPS) * wfn_ref[...]


def _ffn_call(h2, x1, mod3, w_up, w_ffn_dw, w_down, w_final_norm, seq, tm, per_seq_mod):
    n_tok = x1.shape[0]
    tps = seq // tm
    cn = 256
    mod_idx = (lambda i: (i // tps, 0, 0)) if per_seq_mod else (lambda i: (0, 0, 0))
    tok_spec = lambda w: pl.BlockSpec((tm, w), lambda i: (i, 0))
    in_specs = (
        _halo_specs(tm, D_MODEL, n_tok)
        + [tok_spec(D_MODEL), pl.BlockSpec((None, 1, N_MOD * D_MODEL), mod_idx),
           _const_spec((D_MODEL, 2 * D_FF)), _const_spec((FFN_DW_WIDTH, 2 * D_FF)),
           _const_spec((D_FF, D_MODEL)), _const_spec((1, D_MODEL))])
    return pl.pallas_call(
        functools.partial(_ffn_kernel, tm=tm, tps=tps, cn=cn),
        grid=(n_tok // tm,),
        in_specs=in_specs,
        out_specs=tok_spec(D_MODEL),
        out_shape=jax.ShapeDtypeStruct((n_tok, D_MODEL), F32),
        scratch_shapes=[
            pltpu.VMEM((tm + 2 * HALO, D_MODEL), BF16),
            pltpu.VMEM((2, 2, cn // LANES, tm + 2 * HALO, LANES), F32),
            pltpu.VMEM((tm, D_MODEL), F32),
        ],
        compiler_params=_params(1),
        name="ffn",
    )(h2, h2, h2, x1, mod3, w_up, w_ffn_dw, w_down, w_final_norm)


def _tiles(seq):
    tm_in = min(seq, 1024)
    tm = min(seq, 512)
    tq = min(seq, 1024)
    tk = min(seq, 2048)
    heads = N_HEADS if seq <= 256 else 1
    return tm_in, tm, tq, tk, heads


def _trunk_group(x, mod3, per_seq_mod, use_rope, cache, lp):
    n_seq, seq, _ = x.shape
    tm_in, tm, tq, tk, heads = _tiles(seq)
    x2d = x.reshape(n_seq * seq, D_MODEL)
    emit_f32_kv = cache is None
    outs = _in_proj_call(x2d, mod3, lp["w_norm1"], lp["w_in"], seq, tm_in,
                         per_seq_mod, use_rope, emit_f32_kv)
    q, k, v, glu, gates = outs[:5]
    cache_k = cache_v = None
    if cache is not None:
        cache_k, cache_v = cache
    o_n = _attn_call(q, k, v, cache_k, cache_v, lp["lam"], lp["w_head_norm"],
                     n_seq, seq, tq, tk, heads)
    x1, h2 = _merge_call(o_n, glu, gates, x2d, mod3, lp["w_attn_proj"], lp["w_conv_dw"],
                         lp["conv_ln_g"], lp["conv_ln_b"], lp["w_conv_proj"], lp["w_out"],
                         lp["w_norm2"], seq, tm, per_seq_mod)
    y = _ffn_call(h2, x1, mod3, lp["w_up"], lp["w_ffn_dw"], lp["w_down"],
                  lp["w_final_norm"], seq, tm, per_seq_mod)
    return y.reshape(n_seq, seq, D_MODEL), outs[5:]


def kernel(x_prompt, x_sample, cache_k, cache_v, c, c_ctx, w_ada, b_ada, w_norm1, w_in, lambda_q1, lambda_k1, lambda_q2, lambda_k2, w_head_norm, w_attn_proj, w_conv_dw, conv_ln_g, conv_ln_b, w_conv_proj, w_out, w_norm2, w_up, w_ffn_dw, w_down, w_final_norm):
    assert w_in.shape[0] == 1, "single trunk layer"
    n_dec = x_sample.shape[0]
    n_ctx, seq_ctx = x_prompt.shape[0], x_prompt.shape[1]
    past = cache_k.shape[2]

    mod_rows = 16
    cc = jnp.concatenate(
        [c, c_ctx[None, :], jnp.zeros((mod_rows - n_dec - 1, D_MODEL), F32)], axis=0)
    mod = _mod_call(cc, w_ada[0], b_ada)
    mod_lat = mod[:n_dec].reshape(n_dec, 1, N_MOD * D_MODEL)
    mod_ctx = mod[n_dec:n_dec + 1].reshape(1, 1, N_MOD * D_MODEL)

    lp = dict(
        w_norm1=w_norm1, w_in=w_in[0].astype(BF16),
        lam=(lambda_q1, lambda_k1, lambda_q2, lambda_k2), w_head_norm=w_head_norm,
        w_attn_proj=w_attn_proj[0].astype(BF16), w_conv_dw=w_conv_dw[0],
        conv_ln_g=conv_ln_g, conv_ln_b=conv_ln_b,
        w_conv_proj=w_conv_proj[0].astype(BF16), w_out=w_out[0].astype(BF16),
        w_norm2=w_norm2, w_up=w_up[0].astype(BF16), w_ffn_dw=w_ffn_dw[0],
        w_down=w_down[0].astype(BF16), w_final_norm=w_final_norm[None, :])

    y_prompt, (kf, vf) = _trunk_group(x_prompt, mod_ctx, False, False, None, lp)
    cache = (cache_k[:, 0].astype(BF16).reshape(n_dec, past, ATTN_W),
             cache_v[:, 0].astype(BF16).reshape(n_dec, past, ATTN_W))
    y_sample, _ = _trunk_group(x_sample, mod_lat, True, True, cache, lp)
    new_k = kf.reshape(n_ctx, 1, seq_ctx, N_HEADS, HEAD_W)
    new_v = vf.reshape(n_ctx, 1, seq_ctx, N_HEADS, V_DIM)
    return (y_prompt, y_sample, new_k, new_v)
```

```python
import functools

import jax
import jax.numpy as jnp
import numpy as np
from jax import lax
from jax.experimental import pallas as pl
from jax.experimental.pallas import tpu as pltpu

D_MODEL = 1024
N_HEADS = 4
QK_DIM = 64
V_DIM = 2 * QK_DIM
HEAD_W = 2 * QK_DIM
ATTN_W = N_HEADS * V_DIM
CONV_CH = D_MODEL // 2
DW_WIDTH = 31
D_FF = 2816
FFN_DW_WIDTH = 3
GRID_W = 64
ROPE_THETA = 10000.0
EPS = 1e-6
N_MOD = 6
LAM_INIT = 0.8 - 0.6 * float(np.exp(-0.3 * 0))
QK_SCALE = QK_DIM ** -0.5 * float(np.log2(np.e))

MAX_EXCESS = 16.0
LANES = 128
HALO = 16
CONV_PAD = (DW_WIDTH - 1) // 2
CONV_ROWS = 128
VMEM_LIMIT = 56 * 1024 * 1024

F32 = jnp.float32
BF16 = jnp.bfloat16


def _sigmoid(x):
    return 1.0 / (1.0 + jnp.exp(-x))


def _const_spec(shape):
    nd = len(shape)
    return pl.BlockSpec(shape, lambda *_: (0,) * nd, pipeline_mode=pl.Buffered(1))


def _params(n_axes):
    return pltpu.CompilerParams(
        dimension_semantics=("arbitrary",) * n_axes, vmem_limit_bytes=VMEM_LIMIT)


def _mod_kernel(c_ref, w_ref, b_ref, o_ref):
    c = c_ref[...]
    s = c * _sigmoid(c)
    w = w_ref[...]
    s_hi = s.astype(BF16)
    s_lo = (s - s_hi.astype(F32)).astype(BF16)
    w_hi = w.astype(BF16)
    w_lo = (w - w_hi.astype(F32)).astype(BF16)
    acc = jnp.dot(s_hi, w_hi, preferred_element_type=F32)
    acc += jnp.dot(s_hi, w_lo, preferred_element_type=F32)
    acc += jnp.dot(s_lo, w_hi, preferred_element_type=F32)
    o_ref[...] = acc + b_ref[...]


def _mod_call(cc, w_ada, b_ada):
    rows = cc.shape[0]
    n_out = w_ada.shape[1]
    bn = 1536
    return pl.pallas_call(
        _mod_kernel,
        grid=(n_out // bn,),
        in_specs=[
            pl.BlockSpec((rows, D_MODEL), lambda j: (0, 0)),
            pl.BlockSpec((D_MODEL, bn), lambda j: (0, j)),
            pl.BlockSpec((1, bn), lambda j: (0, j)),
        ],
        out_specs=pl.BlockSpec((rows, bn), lambda j: (0, j)),
        out_shape=jax.ShapeDtypeStruct((rows, n_out), F32),
        compiler_params=_params(1),
        name="mod",
    )(cc, w_ada, b_ada)


def _rope_tables(seq):
    t = np.arange(seq)
    row = (t // GRID_W).astype(np.float32).astype(np.float64)
    col = (t % GRID_W).astype(np.float32).astype(np.float64)
    half = QK_DIM // 2
    freqs = ROPE_THETA ** (-np.arange(0, half, 2, dtype=np.float64) / half)
    ar = row[:, None] * freqs
    ac = col[:, None] * freqs
    cos = np.concatenate([np.cos(ar), np.cos(ar), np.cos(ac), np.cos(ac)], axis=1)
    sin = np.concatenate([-np.sin(ar), np.sin(ar), -np.sin(ac), np.sin(ac)], axis=1)
    cos = np.tile(cos, (1, HEAD_W // QK_DIM)).astype(np.float32)
    sin = np.tile(sin, (1, HEAD_W // QK_DIM)).astype(np.float32)
    return jnp.asarray(cos), jnp.asarray(sin)


def _rope(x, cos, sin):
    quarter = QK_DIM // 4
    lane = lax.broadcasted_iota(jnp.int32, (1, HEAD_W), 1)
    fwd = pltpu.roll(x, HEAD_W - quarter, 1)
    bwd = pltpu.roll(x, quarter, 1)
    partner = jnp.where((lane & quarter) == 0, fwd, bwd)
    return x * cos + partner * sin


def _in_proj_kernel(*refs, use_rope, emit_f32_kv):
    it = iter(refs)
    x_ref, mod_ref, wn_ref, w_ref = next(it), next(it), next(it), next(it)
    cos_ref = sin_ref = None
    if use_rope:
        cos_ref, sin_ref = next(it), next(it)
    qt_ref, k_ref, vt_ref, glu_ref, gate_ref = next(it), next(it), next(it), next(it), next(it)
    kf_ref = vf_ref = None
    if emit_f32_kv:
        kf_ref, vf_ref = next(it), next(it)

    x = x_ref[...]
    shift = mod_ref[:, 0:D_MODEL]
    scale = mod_ref[:, D_MODEL:2 * D_MODEL]
    ms = jnp.mean(x * x, axis=-1, keepdims=True)
    h = x * lax.rsqrt(ms + EPS) * wn_ref[...]
    h = h * (1.0 + scale) + shift
    hb = h.astype(BF16)

    def proj(c0, c1):
        return jnp.dot(hb, w_ref[:, c0:c1], preferred_element_type=F32)

    pq = proj(0, ATTN_W)
    pk = proj(ATTN_W, 2 * ATTN_W)
    if emit_f32_kv:
        kf_ref[...] = pk
    for hd in range(N_HEADS):
        sl = slice(hd * HEAD_W, (hd + 1) * HEAD_W)
        qh = pq[:, sl]
        kh = pk[:, sl]
        if use_rope:
            cos = cos_ref[...]
            sin = sin_ref[...]
            qh = _rope(qh, cos, sin)
            kh = _rope(kh, cos, sin)
        qt_ref[sl, :] = (qh * QK_SCALE).T.astype(BF16)
        k_ref[:, sl] = kh.astype(BF16)

    pv = proj(2 * ATTN_W, 3 * ATTN_W)
    if emit_f32_kv:
        vf_ref[...] = pv
    for hd in range(N_HEADS):
        sl = slice(hd * HEAD_W, (hd + 1) * HEAD_W)
        vt_ref[sl, :] = pv[:, sl].T.astype(BF16)

    u0 = 3 * ATTN_W
    pu = proj(u0, u0 + 2 * CONV_CH)
    glu_ref[...] = (pu[:, :CONV_CH] * _sigmoid(pu[:, CONV_CH:])).astype(BF16)

    g0 = u0 + 2 * CONV_CH
    for j in range(2):
        pg = proj(g0 + j * D_MODEL, g0 + (j + 1) * D_MODEL)
        gate_ref[:, j * D_MODEL:(j + 1) * D_MODEL] = _sigmoid(pg).astype(BF16)


def _in_proj_call(x2d, mod3, w_norm1, w_in_bf, seq, tm, per_seq_mod, use_rope, emit_f32_kv):
    n_tok = x2d.shape[0]
    tps = seq // tm
    in_cols = w_in_bf.shape[1]
    mod_idx = (lambda i: (i // tps, 0, 0)) if per_seq_mod else (lambda i: (0, 0, 0))
    in_specs = [
        pl.BlockSpec((tm, D_MODEL), lambda i: (i, 0)),
        pl.BlockSpec((None, 1, N_MOD * D_MODEL), mod_idx),
        _const_spec((1, D_MODEL)),
        _const_spec((D_MODEL, in_cols)),
    ]
    args = [x2d, mod3, w_norm1, w_in_bf]
    if use_rope:
        cos, sin = _rope_tables(seq)
        in_specs += [pl.BlockSpec((tm, HEAD_W), lambda i: (i % tps, 0))] * 2
        args += [cos, sin]
    tok_spec = lambda w: pl.BlockSpec((tm, w), lambda i: (i, 0))
    tr_spec = pl.BlockSpec((ATTN_W, tm), lambda i: (0, i))
    tr_shape = jax.ShapeDtypeStruct((ATTN_W, n_tok), BF16)
    out_specs = [tr_spec, tok_spec(ATTN_W), tr_spec, tok_spec(CONV_CH), tok_spec(2 * D_MODEL)]
    out_shape = [tr_shape, jax.ShapeDtypeStruct((n_tok, ATTN_W), BF16), tr_shape,
                 jax.ShapeDtypeStruct((n_tok, CONV_CH), BF16),
                 jax.ShapeDtypeStruct((n_tok, 2 * D_MODEL), BF16)]
    if emit_f32_kv:
        out_specs += [tok_spec(ATTN_W)] * 2
        out_shape += [jax.ShapeDtypeStruct((n_tok, ATTN_W), F32)] * 2
    return pl.pallas_call(
        functools.partial(_in_proj_kernel, use_rope=use_rope, emit_f32_kv=emit_f32_kv),
        grid=(n_tok // tm,),
        in_specs=in_specs,
        out_specs=out_specs,
        out_shape=out_shape,
        compiler_params=_params(1),
        name="in_proj",
    )(*args)


def _attn_kernel(*refs, tq, tk, n_chunks, heads, use_cache):
    it = iter(refs)
    qt_ref, k_ref, vt_ref = next(it), next(it), next(it)
    ck_ref = cv_ref = None
    if use_cache:
        ck_ref, cv_ref = next(it), next(it)
    lq1, lk1, lq2, lk2, whn_ref = next(it), next(it), next(it), next(it), next(it)
    o_ref = next(it)
    for hd in range(heads):
        _attend_head(slice(hd * HEAD_W, (hd + 1) * HEAD_W), qt_ref, k_ref, vt_ref, ck_ref, cv_ref,
                     (lq1, lk1, lq2, lk2), whn_ref, o_ref, tq, tk, n_chunks)


def _attend_head(hs, qt_ref, k_ref, vt_ref, ck_ref, cv_ref, lam_refs, whn_ref, o_ref, tq, tk, n_chunks):
    lq1, lk1, lq2, lk2 = lam_refs
    qt = qt_ref[hs, :]
    zero = jnp.zeros((QK_DIM, tq), BF16)
    qs = jnp.concatenate(
        [jnp.concatenate([qt[0:QK_DIM, :], zero], axis=0),
         jnp.concatenate([zero, qt[QK_DIM:HEAD_W, :]], axis=0)], axis=1)

    chunks = [(k_ref[j * tk:(j + 1) * tk, hs], vt_ref[hs, j * tk:(j + 1) * tk])
              for j in range(n_chunks)]
    if ck_ref is not None:
        chunks.insert(0, (ck_ref[0, :, hs], cv_ref[0, :, hs].astype(F32).T.astype(BF16)))

    def finalize(l, acc):
        ot = acc / l
        lam = (jnp.exp(jnp.sum(lq1[...] * lk1[...], axis=-1, keepdims=True))
               - jnp.exp(jnp.sum(lq2[...] * lk2[...], axis=-1, keepdims=True)) + LAM_INIT)
        o = (ot[:, 0:tq] - lam * ot[:, tq:2 * tq]).T
        ms = jnp.mean(o * o, axis=-1, keepdims=True)
        o = o * lax.rsqrt(ms + EPS) * whn_ref[...] * (1.0 - LAM_INIT)
        o_ref[:, hs] = o.astype(BF16)

    kb, vtb = chunks[0]
    s = jnp.dot(kb, qs, preferred_element_type=F32)
    ref_row = jnp.max(s, axis=0, keepdims=True).astype(BF16).astype(F32)
    p = jnp.exp2(s - ref_row)
    l = jnp.sum(p, axis=0, keepdims=True)
    acc = jnp.dot(vtb, p.astype(BF16), preferred_element_type=F32)
    if len(chunks) == 1:
        finalize(l, acc)
        return

    row = lax.broadcasted_iota(jnp.int32, (16, 2 * tq), 0)
    neg_ref = jnp.where(row == 0, -ref_row, 0.0).astype(BF16)
    qs_aug = jnp.concatenate(
        [qs, neg_ref, jnp.zeros((HEAD_W - 16, 2 * tq), BF16)], axis=0)
    excess = jnp.zeros((1, 2 * tq), F32)
    for kb, vtb in chunks[1:]:
        k_aug = jnp.concatenate([kb, jnp.ones((kb.shape[0], HEAD_W), BF16)], axis=1)
        s = jnp.dot(k_aug, qs_aug, preferred_element_type=F32)
        p = jnp.exp2(s)
        excess = jnp.maximum(excess, jnp.max(s, axis=0, keepdims=True))
        l = l + jnp.sum(p, axis=0, keepdims=True)
        acc = acc + jnp.dot(vtb, p.astype(BF16), preferred_element_type=F32)
    in_range = jnp.max(excess) <= MAX_EXCESS

    @pl.when(in_range)
    def _():
        finalize(l, acc)

    @pl.when(jnp.logical_not(in_range))
    def _():
        def update(state, kb, vtb):
            m_prev, l_prev, acc = state
            s = jnp.dot(kb, qs, preferred_element_type=F32)
            m_new = jnp.maximum(m_prev, jnp.max(s, axis=0, keepdims=True))
            alpha = jnp.exp2(m_prev - m_new)
            p = jnp.exp2(s - m_new)
            l_new = alpha * l_prev + jnp.sum(p, axis=0, keepdims=True)
            acc = alpha * acc + jnp.dot(vtb, p.astype(BF16), preferred_element_type=F32)
            return m_new, l_new, acc

        state = (jnp.full((1, 2 * tq), -jnp.inf, F32), jnp.zeros((1, 2 * tq), F32),
                 jnp.zeros((V_DIM, 2 * tq), F32))
        for kb, vtb in chunks:
            state = update(state, kb, vtb)
        finalize(state[1], state[2])


def _attn_call(qt, k, vt, cache_k, cache_v, lam_params, w_head_norm, n_seq, seq, tq, tk, heads):
    n_tok = k.shape[0]
    qps = seq // tq
    use_cache = cache_k is not None
    hw = heads * HEAD_W
    in_specs = [
        pl.BlockSpec((hw, tq), lambda b, h, i: (h, b * qps + i)),
        pl.BlockSpec((seq, hw), lambda b, h, i: (b, h)),
        pl.BlockSpec((hw, seq), lambda b, h, i: (h, b)),
    ]
    args = [qt, k, vt]
    if use_cache:
        past = cache_k.shape[1]
        in_specs += [pl.BlockSpec((1, past, hw), lambda b, h, i: (b, 0, h))] * 2
        args += [cache_k, cache_v]
    in_specs += [_const_spec((1, QK_DIM))] * 4 + [_const_spec((1, V_DIM))]
    args += list(lam_params) + [w_head_norm]
    return pl.pallas_call(
        functools.partial(_attn_kernel, tq=tq, tk=tk, n_chunks=seq // tk, heads=heads,
                          use_cache=use_cache),
        grid=(n_seq, N_HEADS // heads, qps),
        in_specs=in_specs,
        out_specs=pl.BlockSpec((tq, hw), lambda b, h, i: (b * qps + i, h)),
        out_shape=jax.ShapeDtypeStruct((n_tok, ATTN_W), BF16),
        compiler_params=_params(3),
        name="attn",
    )(*args)


def _halo_specs(tm, width, n_tok):
    r = tm // HALO
    last = n_tok // HALO - 1
    return [
        pl.BlockSpec((tm, width), lambda i: (i, 0)),
        pl.BlockSpec((HALO, width), lambda i: (jnp.maximum(i * r - 1, 0), 0)),
        pl.BlockSpec((HALO, width), lambda i: (jnp.minimum((i + 1) * r, last), 0)),
    ]


def _fill_ext(ext_ref, main_ref, prev_ref, next_ref, tm, tps):
    j = pl.program_id(0) % tps
    prev = prev_ref[...].astype(ext_ref.dtype)
    nxt = next_ref[...].astype(ext_ref.dtype)
    ext_ref[0:HALO, :] = jnp.where(j > 0, prev, jnp.zeros_like(prev))
    ext_ref[HALO:HALO + tm, :] = main_ref[...].astype(ext_ref.dtype)
    ext_ref[HALO + tm:, :] = jnp.where(j < tps - 1, nxt, jnp.zeros_like(nxt))


def _merge_kernel(o_ref, glu_ref, glu_prev, glu_next, gate_ref, x_ref, mod_ref,
                  wap_ref, wdw_ref, lng_ref, lnb_ref, wcp_ref, wout_ref, wn2_ref,
                  x1_ref, h2_ref, ext_ref, conv_ref, *, tm, tps):
    j = pl.program_id(0) % tps
    for s in range(CONV_CH // LANES):
        cols = slice(s * LANES, (s + 1) * LANES)
        prev = glu_prev[:, cols].astype(F32)
        nxt = glu_next[:, cols].astype(F32)
        ext_ref[s, 0:HALO, :] = jnp.where(j > 0, prev, jnp.zeros_like(prev))
        ext_ref[s, HALO:HALO + tm, :] = glu_ref[:, cols].astype(F32)
        ext_ref[s, HALO + tm:, :] = jnp.where(j < tps - 1, nxt, jnp.zeros_like(nxt))
    for s in range(CONV_CH // LANES):
        cols = slice(s * LANES, (s + 1) * LANES)
        for r0 in range(0, tm, CONV_ROWS):
            acc = None
            for t in range(DW_WIDTH):
                tap = (ext_ref[s, pl.ds(HALO - CONV_PAD + r0 + t, CONV_ROWS, stride=1), :]
                       * wdw_ref[t:t + 1, cols])
                acc = tap if acc is None else acc + tap
            conv_ref[r0:r0 + CONV_ROWS, cols] = acc
    acc = conv_ref[...]
    mu = jnp.mean(acc, axis=-1, keepdims=True)
    d = acc - mu
    var = jnp.mean(d * d, axis=-1, keepdims=True)
    y = d * lax.rsqrt(var + EPS) * lng_ref[...] + lnb_ref[...]
    cv = (y * _sigmoid(y)).astype(BF16)
    conv_out = jnp.dot(cv, wcp_ref[...], preferred_element_type=F32)
    attn_out = jnp.dot(o_ref[...], wap_ref[...], preferred_element_type=F32)
    merged = (gate_ref[:, 0:D_MODEL].astype(F32) * attn_out
              + gate_ref[:, D_MODEL:2 * D_MODEL].astype(F32) * conv_out)
    mix = jnp.dot(merged.astype(BF16), wout_ref[...], preferred_element_type=F32)
    gate1 = mod_ref[:, 2 * D_MODEL:3 * D_MODEL]
    shift2 = mod_ref[:, 3 * D_MODEL:4 * D_MODEL]
    scale2 = mod_ref[:, 4 * D_MODEL:5 * D_MODEL]
    x1 = x_ref[...] + gate1 * mix
    x1_ref[...] = x1
    ms = jnp.mean(x1 * x1, axis=-1, keepdims=True)
    h2 = x1 * lax.rsqrt(ms + EPS) * wn2_ref[...]
    h2_ref[...] = (h2 * (1.0 + scale2) + shift2).astype(BF16)


def _merge_call(o_n, glu, gates, x2d, mod3, w_attn_proj, w_conv_dw, ln_g, ln_b,
                w_conv_proj, w_out, w_norm2, seq, tm, per_seq_mod):
    n_tok = x2d.shape[0]
    tps = seq // tm
    mod_idx = (lambda i: (i // tps, 0, 0)) if per_seq_mod else (lambda i: (0, 0, 0))
    tok_spec = lambda w: pl.BlockSpec((tm, w), lambda i: (i, 0))
    in_specs = (
        [tok_spec(ATTN_W)] + _halo_specs(tm, CONV_CH, n_tok)
        + [tok_spec(2 * D_MODEL), tok_spec(D_MODEL),
           pl.BlockSpec((None, 1, N_MOD * D_MODEL), mod_idx),
           _const_spec((ATTN_W, D_MODEL)), _const_spec((DW_WIDTH, CONV_CH)),
           _const_spec((1, CONV_CH)), _const_spec((1, CONV_CH)),
           _const_spec((CONV_CH, D_MODEL)), _const_spec((D_MODEL, D_MODEL)),
           _const_spec((1, D_MODEL))])
    return pl.pallas_call(
        functools.partial(_merge_kernel, tm=tm, tps=tps),
        grid=(n_tok // tm,),
        in_specs=in_specs,
        out_specs=[tok_spec(D_MODEL), tok_spec(D_MODEL)],
        out_shape=[jax.ShapeDtypeStruct((n_tok, D_MODEL), F32),
                   jax.ShapeDtypeStruct((n_tok, D_MODEL), BF16)],
        scratch_shapes=[pltpu.VMEM((CONV_CH // LANES, tm + 2 * HALO, LANES), F32),
                        pltpu.VMEM((tm, CONV_CH), F32)],
        compiler_params=_params(1),
        name="merge",
    )(o_n, glu, glu, glu, gates, x2d, mod3, w_attn_proj, w_conv_dw, ln_g, ln_b,
      w_conv_proj, w_out, w_norm2)


def _ffn_kernel(h_ref, h_prev, h_next, x1_ref, mod_ref, wup_ref, wdw_ref, wdn_ref, wfn_ref,
                y_ref, lhs_ref, u_ref, acc_ref, *, tm, tps, cn):
    _fill_ext(lhs_ref, h_ref, h_prev, h_next, tm, tps)
    lhs = lhs_ref[...]
    n_slabs = cn // LANES

    def up_proj(par, half, c0):
        res = jnp.dot(lhs, wup_ref[:, c0:c0 + cn], preferred_element_type=F32)
        for s in range(n_slabs):
            u_ref[par, half, s] = res[:, s * LANES:(s + 1) * LANES]

    def conv(par, half, c0):
        outs = []
        for s in range(n_slabs):
            cols = slice(c0 + s * LANES, c0 + (s + 1) * LANES)
            taps = [u_ref[par, half, s, pl.ds(HALO - 1 + j, tm, stride=1), :] * wdw_ref[j:j + 1, cols]
                    for j in range(FFN_DW_WIDTH)]
            outs.append(taps[0] + taps[1] + taps[2])
        return jnp.concatenate(outs, axis=1)

    n_chunks = D_FF // cn
    up_proj(0, 0, 0)
    up_proj(0, 1, D_FF)
    for c in range(n_chunks):
        a0 = c * cn
        b0 = D_FF + c * cn
        par = c % 2
        if c + 1 < n_chunks:
            up_proj(1 - par, 0, a0 + cn)
            up_proj(1 - par, 1, b0 + cn)
        a = conv(par, 0, a0)
        b = conv(par, 1, b0)
        act = (a * _sigmoid(a) * b).astype(BF16)
        down = jnp.dot(act, wdn_ref[a0:a0 + cn, :], preferred_element_type=F32)
        if c == 0:
            acc_ref[...] = down
        else:
            acc_ref[...] += down

    gate2 = mod_ref[:, 5 * D_MODEL:6 * D_MODEL]
    y = x1_ref[...] + gate2 * acc_ref[...]
    ms = jnp.mean(y * y, axis=-1, keepdims=True)
    y_ref[...] = y * lax.rsqrt(ms + EPS) * wfn_ref[...]


def _ffn_call(h2, x1, mod3, w_up, w_ffn_dw, w_down, w_final_norm, seq, tm, per_seq_mod):
    n_tok = x1.shape[0]
    tps = seq // tm
    cn = 256
    mod_idx = (lambda i: (i // tps, 0, 0)) if per_seq_mod else (lambda i: (0, 0, 0))
    tok_spec = lambda w: pl.BlockSpec((tm, w), lambda i: (i, 0))
    in_specs = (
        _halo_specs(tm, D_MODEL, n_tok)
        + [tok_spec(D_MODEL), pl.BlockSpec((None, 1, N_MOD * D_MODEL), mod_idx),
           _const_spec((D_MODEL, 2 * D_FF)), _const_spec((FFN_DW_WIDTH, 2 * D_FF)),
           _const_spec((D_FF, D_MODEL)), _const_spec((1, D_MODEL))])
    return pl.pallas_call(
        functools.partial(_ffn_kernel, tm=tm, tps=tps, cn=cn),
        grid=(n_tok // tm,),
        in_specs=in_specs,
        out_specs=tok_spec(D_MODEL),
        out_shape=jax.ShapeDtypeStruct((n_tok, D_MODEL), F32),
        scratch_shapes=[
            pltpu.VMEM((tm + 2 * HALO, D_MODEL), BF16),
            pltpu.VMEM((2, 2, cn // LANES, tm + 2 * HALO, LANES), F32),
            pltpu.VMEM((tm, D_MODEL), F32),
        ],
        compiler_params=_params(1),
        name="ffn",
    )(h2, h2, h2, x1, mod3, w_up, w_ffn_dw, w_down, w_final_norm)


def _tiles(seq):
    tm_in = min(seq, 1024)
    tm = min(seq, 512)
    tq = min(seq, 1024)
    tk = min(seq, 2048)
    heads = N_HEADS if seq <= 256 else 1
    return tm_in, tm, tq, tk, heads


def _trunk_group(x, mod3, per_seq_mod, use_rope, cache, lp):
    n_seq, seq, _ = x.shape
    tm_in, tm, tq, tk, heads = _tiles(seq)
    x2d = x.reshape(n_seq * seq, D_MODEL)
    emit_f32_kv = cache is None
    outs = _in_proj_call(x2d, mod3, lp["w_norm1"], lp["w_in"], seq, tm_in,
                         per_seq_mod, use_rope, emit_f32_kv)
    q, k, v, glu, gates = outs[:5]
    cache_k = cache_v = None
    if cache is not None:
        cache_k, cache_v = cache
    o_n = _attn_call(q, k, v, cache_k, cache_v, lp["lam"], lp["w_head_norm"],
                     n_seq, seq, tq, tk, heads)
    x1, h2 = _merge_call(o_n, glu, gates, x2d, mod3, lp["w_attn_proj"], lp["w_conv_dw"],
                         lp["conv_ln_g"], lp["conv_ln_b"], lp["w_conv_proj"], lp["w_out"],
                         lp["w_norm2"], seq, tm_in, per_seq_mod)
    y = _ffn_call(h2, x1, mod3, lp["w_up"], lp["w_ffn_dw"], lp["w_down"],
                  lp["w_final_norm"], seq, tm, per_seq_mod)
    return y.reshape(n_seq, seq, D_MODEL), outs[5:]


def kernel(x_prompt, x_sample, cache_k, cache_v, c, c_ctx, w_ada, b_ada, w_norm1, w_in, lambda_q1, lambda_k1, lambda_q2, lambda_k2, w_head_norm, w_attn_proj, w_conv_dw, conv_ln_g, conv_ln_b, w_conv_proj, w_out, w_norm2, w_up, w_ffn_dw, w_down, w_final_norm):
    assert w_in.shape[0] == 1, "single trunk layer"
    n_dec = x_sample.shape[0]
    n_ctx, seq_ctx = x_prompt.shape[0], x_prompt.shape[1]
    past = cache_k.shape[2]

    mod_rows = 16
    cc = jnp.concatenate(
        [c, c_ctx[None, :], jnp.zeros((mod_rows - n_dec - 1, D_MODEL), F32)], axis=0)
    mod = _mod_call(cc, w_ada[0], b_ada)
    mod_lat = mod[:n_dec].reshape(n_dec, 1, N_MOD * D_MODEL)
    mod_ctx = mod[n_dec:n_dec + 1].reshape(1, 1, N_MOD * D_MODEL)

    lp = dict(
        w_norm1=w_norm1, w_in=w_in[0].astype(BF16),
        lam=(lambda_q1, lambda_k1, lambda_q2, lambda_k2), w_head_norm=w_head_norm,
        w_attn_proj=w_attn_proj[0].astype(BF16), w_conv_dw=w_conv_dw[0],
        conv_ln_g=conv_ln_g, conv_ln_b=conv_ln_b,
        w_conv_proj=w_conv_proj[0].astype(BF16), w_out=w_out[0].astype(BF16),
        w_norm2=w_norm2, w_up=w_up[0].astype(BF16), w_ffn_dw=w_ffn_dw[0],
        w_down=w_down[0].astype(BF16), w_final_norm=w_final_norm[None, :])

    y_prompt, (kf, vf) = _trunk_group(x_prompt, mod_ctx, False, False, None, lp)
    cache = (cache_k[:, 0].astype(BF16).reshape(n_dec, past, ATTN_W),
             cache_v[:, 0].astype(BF16).reshape(n_dec, past, ATTN_W))
    y_sample, _ = _trunk_group(x_sample, mod_lat, True, True, cache, lp)
    new_k = kf.reshape(n_ctx, 1, seq_ctx, N_HEADS, HEAD_W)
    new_v = vf.reshape(n_ctx, 1, seq_ctx, N_HEADS, V_DIM)
    return (y_prompt, y_sample, new_k, new_v)
```

```python
import functools

import jax
import jax.numpy as jnp
import numpy as np
from jax import lax
from jax.experimental import pallas as pl
from jax.experimental.pallas import tpu as pltpu

D_MODEL = 1024
N_HEADS = 4
QK_DIM = 64
V_DIM = 2 * QK_DIM
HEAD_W = 2 * QK_DIM
ATTN_W = N_HEADS * V_DIM
CONV_CH = D_MODEL // 2
DW_WIDTH = 31
D_FF = 2816
FFN_DW_WIDTH = 3
GRID_W = 64
ROPE_THETA = 10000.0
EPS = 1e-6
N_MOD = 6
LAM_INIT = 0.8 - 0.6 * float(np.exp(-0.3 * 0))
QK_SCALE = QK_DIM ** -0.5 * float(np.log2(np.e))

MAX_EXCESS = 16.0
LANES = 128
HALO = 16
CONV_PAD = (DW_WIDTH - 1) // 2
CONV_ROWS = 128
VMEM_LIMIT = 56 * 1024 * 1024

F32 = jnp.float32
BF16 = jnp.bfloat16


def _sigmoid(x):
    return 1.0 / (1.0 + jnp.exp(-x))


def _const_spec(shape):
    nd = len(shape)
    return pl.BlockSpec(shape, lambda *_: (0,) * nd, pipeline_mode=pl.Buffered(1))


def _params(n_axes):
    return pltpu.CompilerParams(
        dimension_semantics=("arbitrary",) * n_axes, vmem_limit_bytes=VMEM_LIMIT)


def _mod_kernel(c_ref, w_ref, b_ref, o_ref):
    c = c_ref[...]
    s = c * _sigmoid(c)
    w = w_ref[...]
    s_hi = s.astype(BF16)
    s_lo = (s - s_hi.astype(F32)).astype(BF16)
    w_hi = w.astype(BF16)
    w_lo = (w - w_hi.astype(F32)).astype(BF16)
    acc = jnp.dot(s_hi, w_hi, preferred_element_type=F32)
    acc += jnp.dot(s_hi, w_lo, preferred_element_type=F32)
    acc += jnp.dot(s_lo, w_hi, preferred_element_type=F32)
    o_ref[...] = acc + b_ref[...]


def _mod_call(cc, w_ada, b_ada):
    rows = cc.shape[0]
    n_out = w_ada.shape[1]
    bn = 1536
    return pl.pallas_call(
        _mod_kernel,
        grid=(n_out // bn,),
        in_specs=[
            pl.BlockSpec((rows, D_MODEL), lambda j: (0, 0)),
            pl.BlockSpec((D_MODEL, bn), lambda j: (0, j)),
            pl.BlockSpec((1, bn), lambda j: (0, j)),
        ],
        out_specs=pl.BlockSpec((rows, bn), lambda j: (0, j)),
        out_shape=jax.ShapeDtypeStruct((rows, n_out), F32),
        compiler_params=_params(1),
        name="mod",
    )(cc, w_ada, b_ada)


def _rope_tables(seq):
    t = np.arange(seq)
    row = (t // GRID_W).astype(np.float32).astype(np.float64)
    col = (t % GRID_W).astype(np.float32).astype(np.float64)
    half = QK_DIM // 2
    freqs = ROPE_THETA ** (-np.arange(0, half, 2, dtype=np.float64) / half)
    ar = row[:, None] * freqs
    ac = col[:, None] * freqs
    cos = np.concatenate([np.cos(ar), np.cos(ar), np.cos(ac), np.cos(ac)], axis=1)
    sin = np.concatenate([-np.sin(ar), np.sin(ar), -np.sin(ac), np.sin(ac)], axis=1)
    cos = np.tile(cos, (1, HEAD_W // QK_DIM)).astype(np.float32)
    sin = np.tile(sin, (1, HEAD_W // QK_DIM)).astype(np.float32)
    return jnp.asarray(cos), jnp.asarray(sin)


def _rope(x, cos, sin):
    quarter = QK_DIM // 4
    lane = lax.broadcasted_iota(jnp.int32, (1, HEAD_W), 1)
    fwd = pltpu.roll(x, HEAD_W - quarter, 1)
    bwd = pltpu.roll(x, quarter, 1)
    partner = jnp.where((lane & quarter) == 0, fwd, bwd)
    return x * cos + partner * sin


def _in_proj_kernel(*refs, use_rope, emit_f32_kv):
    it = iter(refs)
    x_ref, mod_ref, wn_ref, w_ref = next(it), next(it), next(it), next(it)
    cos_ref = sin_ref = None
    if use_rope:
        cos_ref, sin_ref = next(it), next(it)
    qt_ref, k_ref, vt_ref, glu_ref, gate_ref = next(it), next(it), next(it), next(it), next(it)
    kf_ref = vf_ref = None
    if emit_f32_kv:
        kf_ref, vf_ref = next(it), next(it)

    x = x_ref[...]
    shift = mod_ref[:, 0:D_MODEL]
    scale = mod_ref[:, D_MODEL:2 * D_MODEL]
    ms = jnp.mean(x * x, axis=-1, keepdims=True)
    h = x * lax.rsqrt(ms + EPS) * wn_ref[...]
    h = h * (1.0 + scale) + shift
    hb = h.astype(BF16)

    def proj(c0, c1):
        return jnp.dot(hb, w_ref[:, c0:c1], preferred_element_type=F32)

    pq = proj(0, ATTN_W)
    pk = proj(ATTN_W, 2 * ATTN_W)
    if emit_f32_kv:
        kf_ref[...] = pk
    for hd in range(N_HEADS):
        sl = slice(hd * HEAD_W, (hd + 1) * HEAD_W)
        qh = pq[:, sl]
        kh = pk[:, sl]
        if use_rope:
            cos = cos_ref[...]
            sin = sin_ref[...]
            qh = _rope(qh, cos, sin)
            kh = _rope(kh, cos, sin)
        qt_ref[sl, :] = (qh * QK_SCALE).T.astype(BF16)
        k_ref[:, sl] = kh.astype(BF16)

    pv = proj(2 * ATTN_W, 3 * ATTN_W)
    if emit_f32_kv:
        vf_ref[...] = pv
    for hd in range(N_HEADS):
        sl = slice(hd * HEAD_W, (hd + 1) * HEAD_W)
        vt_ref[sl, :] = pv[:, sl].T.astype(BF16)

    u0 = 3 * ATTN_W
    pu = proj(u0, u0 + 2 * CONV_CH)
    glu_ref[...] = (pu[:, :CONV_CH] * _sigmoid(pu[:, CONV_CH:])).astype(BF16)

    g0 = u0 + 2 * CONV_CH
    for j in range(2):
        pg = proj(g0 + j * D_MODEL, g0 + (j + 1) * D_MODEL)
        gate_ref[:, j * D_MODEL:(j + 1) * D_MODEL] = _sigmoid(pg).astype(BF16)


def _in_proj_call(x2d, mod3, w_norm1, w_in_bf, seq, tm, per_seq_mod, use_rope, emit_f32_kv):
    n_tok = x2d.shape[0]
    tps = seq // tm
    in_cols = w_in_bf.shape[1]
    mod_idx = (lambda i: (i // tps, 0, 0)) if per_seq_mod else (lambda i: (0, 0, 0))
    in_specs = [
        pl.BlockSpec((tm, D_MODEL), lambda i: (i, 0)),
        pl.BlockSpec((None, 1, N_MOD * D_MODEL), mod_idx),
        _const_spec((1, D_MODEL)),
        _const_spec((D_MODEL, in_cols)),
    ]
    args = [x2d, mod3, w_norm1, w_in_bf]
    if use_rope:
        cos, sin = _rope_tables(seq)
        in_specs += [pl.BlockSpec((tm, HEAD_W), lambda i: (i % tps, 0))] * 2
        args += [cos, sin]
    tok_spec = lambda w: pl.BlockSpec((tm, w), lambda i: (i, 0))
    tr_spec = pl.BlockSpec((ATTN_W, tm), lambda i: (0, i))
    tr_shape = jax.ShapeDtypeStruct((ATTN_W, n_tok), BF16)
    out_specs = [tr_spec, tok_spec(ATTN_W), tr_spec, tok_spec(CONV_CH), tok_spec(2 * D_MODEL)]
    out_shape = [tr_shape, jax.ShapeDtypeStruct((n_tok, ATTN_W), BF16), tr_shape,
                 jax.ShapeDtypeStruct((n_tok, CONV_CH), BF16),
                 jax.ShapeDtypeStruct((n_tok, 2 * D_MODEL), BF16)]
    if emit_f32_kv:
        out_specs += [tok_spec(ATTN_W)] * 2
        out_shape += [jax.ShapeDtypeStruct((n_tok, ATTN_W), F32)] * 2
    return pl.pallas_call(
        functools.partial(_in_proj_kernel, use_rope=use_rope, emit_f32_kv=emit_f32_kv),
        grid=(n_tok // tm,),
        in_specs=in_specs,
        out_specs=out_specs,
        out_shape=out_shape,
        compiler_params=_params(1),
        name="in_proj",
    )(*args)


def _attn_kernel(*refs, tq, tk, n_chunks, heads, use_cache):
    it = iter(refs)
    qt_ref, k_ref, vt_ref = next(it), next(it), next(it)
    ck_ref = cv_ref = None
    if use_cache:
        ck_ref, cv_ref = next(it), next(it)
    lq1, lk1, lq2, lk2, whn_ref = next(it), next(it), next(it), next(it), next(it)
    o_ref = next(it)
    for hd in range(heads):
        _attend_head(slice(hd * HEAD_W, (hd + 1) * HEAD_W), qt_ref, k_ref, vt_ref, ck_ref, cv_ref,
                     (lq1, lk1, lq2, lk2), whn_ref, o_ref, tq, tk, n_chunks)


def _attend_head(hs, qt_ref, k_ref, vt_ref, ck_ref, cv_ref, lam_refs, whn_ref, o_ref, tq, tk, n_chunks):
    lq1, lk1, lq2, lk2 = lam_refs
    qt = qt_ref[hs, :]
    zero = jnp.zeros((QK_DIM, tq), BF16)
    qs = jnp.concatenate(
        [jnp.concatenate([qt[0:QK_DIM, :], zero], axis=0),
         jnp.concatenate([zero, qt[QK_DIM:HEAD_W, :]], axis=0)], axis=1)

    chunks = [(k_ref[j * tk:(j + 1) * tk, hs], vt_ref[hs, j * tk:(j + 1) * tk])
              for j in range(n_chunks)]
    if ck_ref is not None:
        chunks.insert(0, (ck_ref[0, :, hs], cv_ref[0, :, hs].astype(F32).T.astype(BF16)))

    def finalize(l, acc):
        ot = acc / l
        lam = (jnp.exp(jnp.sum(lq1[...] * lk1[...], axis=-1, keepdims=True))
               - jnp.exp(jnp.sum(lq2[...] * lk2[...], axis=-1, keepdims=True)) + LAM_INIT)
        o = (ot[:, 0:tq] - lam * ot[:, tq:2 * tq]).T
        ms = jnp.mean(o * o, axis=-1, keepdims=True)
        o = o * lax.rsqrt(ms + EPS) * whn_ref[...] * (1.0 - LAM_INIT)
        o_ref[:, hs] = o.astype(BF16)

    kb, vtb = chunks[0]
    s = jnp.dot(kb, qs, preferred_element_type=F32)
    ref_row = jnp.max(s, axis=0, keepdims=True).astype(BF16).astype(F32)
    p = jnp.exp2(s - ref_row)
    l = jnp.sum(p, axis=0, keepdims=True)
    acc = jnp.dot(vtb, p.astype(BF16), preferred_element_type=F32)
    if len(chunks) == 1:
        finalize(l, acc)
        return

    row = lax.broadcasted_iota(jnp.int32, (16, 2 * tq), 0)
    neg_ref = jnp.where(row == 0, -ref_row, 0.0).astype(BF16)
    qs_aug = jnp.concatenate(
        [qs, neg_ref, jnp.zeros((HEAD_W - 16, 2 * tq), BF16)], axis=0)
    excess = jnp.zeros((1, 2 * tq), F32)
    for kb, vtb in chunks[1:]:
        k_aug = jnp.concatenate([kb, jnp.ones((kb.shape[0], HEAD_W), BF16)], axis=1)
        s = jnp.dot(k_aug, qs_aug, preferred_element_type=F32)
        p = jnp.exp2(s)
        excess = jnp.maximum(excess, jnp.max(s, axis=0, keepdims=True))
        l = l + jnp.sum(p, axis=0, keepdims=True)
        acc = acc + jnp.dot(vtb, p.astype(BF16), preferred_element_type=F32)
    in_range = jnp.max(excess) <= MAX_EXCESS

    @pl.when(in_range)
    def _():
        finalize(l, acc)

    @pl.when(jnp.logical_not(in_range))
    def _():
        def update(state, kb, vtb):
            m_prev, l_prev, acc = state
            s = jnp.dot(kb, qs, preferred_element_type=F32)
            m_new = jnp.maximum(m_prev, jnp.max(s, axis=0, keepdims=True))
            alpha = jnp.exp2(m_prev - m_new)
            p = jnp.exp2(s - m_new)
            l_new = alpha * l_prev + jnp.sum(p, axis=0, keepdims=True)
            acc = alpha * acc + jnp.dot(vtb, p.astype(BF16), preferred_element_type=F32)
            return m_new, l_new, acc

        state = (jnp.full((1, 2 * tq), -jnp.inf, F32), jnp.zeros((1, 2 * tq), F32),
                 jnp.zeros((V_DIM, 2 * tq), F32))
        for kb, vtb in chunks:
            state = update(state, kb, vtb)
        finalize(state[1], state[2])


def _attn_call(qt, k, vt, cache_k, cache_v, lam_params, w_head_norm, n_seq, seq, tq, tk, heads):
    n_tok = k.shape[0]
    qps = seq // tq
    use_cache = cache_k is not None
    hw = heads * HEAD_W
    in_specs = [
        pl.BlockSpec((hw, tq), lambda b, h, i: (h, b * qps + i)),
        pl.BlockSpec((seq, hw), lambda b, h, i: (b, h)),
        pl.BlockSpec((hw, seq), lambda b, h, i: (h, b)),
    ]
    args = [qt, k, vt]
    if use_cache:
        past = cache_k.shape[1]
        in_specs += [pl.BlockSpec((1, past, hw), lambda b, h, i: (b, 0, h))] * 2
        args += [cache_k, cache_v]
    in_specs += [_const_spec((1, QK_DIM))] * 4 + [_const_spec((1, V_DIM))]
    args += list(lam_params) + [w_head_norm]
    return pl.pallas_call(
        functools.partial(_attn_kernel, tq=tq, tk=tk, n_chunks=seq // tk, heads=heads,
                          use_cache=use_cache),
        grid=(n_seq, N_HEADS // heads, qps),
        in_specs=in_specs,
        out_specs=pl.BlockSpec((tq, hw), lambda b, h, i: (b * qps + i, h)),
        out_shape=jax.ShapeDtypeStruct((n_tok, ATTN_W), BF16),
        compiler_params=_params(3),
        name="attn",
    )(*args)


def _halo_specs(tm, width, n_tok):
    r = tm // HALO
    last = n_tok // HALO - 1
    return [
        pl.BlockSpec((tm, width), lambda i: (i, 0)),
        pl.BlockSpec((HALO, width), lambda i: (jnp.maximum(i * r - 1, 0), 0)),
        pl.BlockSpec((HALO, width), lambda i: (jnp.minimum((i + 1) * r, last), 0)),
    ]


def _fill_ext(ext_ref, main_ref, prev_ref, next_ref, tm, tps):
    j = pl.program_id(0) % tps
    prev = prev_ref[...].astype(ext_ref.dtype)
    nxt = next_ref[...].astype(ext_ref.dtype)
    ext_ref[0:HALO, :] = jnp.where(j > 0, prev, jnp.zeros_like(prev))
    ext_ref[HALO:HALO + tm, :] = main_ref[...].astype(ext_ref.dtype)
    ext_ref[HALO + tm:, :] = jnp.where(j < tps - 1, nxt, jnp.zeros_like(nxt))


def _merge_kernel(o_ref, glu_ref, glu_prev, glu_next, gate_ref, x_ref, mod_ref,
                  wap_ref, wdw_ref, lng_ref, lnb_ref, wcp_ref, wout_ref, wn2_ref,
                  x1_ref, h2_ref, ext_ref, conv_ref, *, tm, tps):
    j = pl.program_id(0) % tps
    for s in range(CONV_CH // LANES):
        cols = slice(s * LANES, (s + 1) * LANES)
        prev = glu_prev[:, cols].astype(F32)
        nxt = glu_next[:, cols].astype(F32)
        ext_ref[s, 0:HALO, :] = jnp.where(j > 0, prev, jnp.zeros_like(prev))
        ext_ref[s, HALO:HALO + tm, :] = glu_ref[:, cols].astype(F32)
        ext_ref[s, HALO + tm:, :] = jnp.where(j < tps - 1, nxt, jnp.zeros_like(nxt))
    for s in range(CONV_CH // LANES):
        cols = slice(s * LANES, (s + 1) * LANES)
        for r0 in range(0, tm, CONV_ROWS):
            acc = None
            for t in range(DW_WIDTH):
                tap = (ext_ref[s, pl.ds(HALO - CONV_PAD + r0 + t, CONV_ROWS, stride=1), :]
                       * wdw_ref[t:t + 1, cols])
                acc = tap if acc is None else acc + tap
            conv_ref[r0:r0 + CONV_ROWS, cols] = acc
    acc = conv_ref[...]
    mu = jnp.mean(acc, axis=-1, keepdims=True)
    d = acc - mu
    var = jnp.mean(d * d, axis=-1, keepdims=True)
    y = d * lax.rsqrt(var + EPS) * lng_ref[...] + lnb_ref[...]
    cv = (y * _sigmoid(y)).astype(BF16)
    conv_out = jnp.dot(cv, wcp_ref[...], preferred_element_type=F32)
    attn_out = jnp.dot(o_ref[...], wap_ref[...], preferred_element_type=F32)
    merged = (gate_ref[:, 0:D_MODEL].astype(F32) * attn_out
              + gate_ref[:, D_MODEL:2 * D_MODEL].astype(F32) * conv_out)
    mix = jnp.dot(merged.astype(BF16), wout_ref[...], preferred_element_type=F32)
    gate1 = mod_ref[:, 2 * D_MODEL:3 * D_MODEL]
    shift2 = mod_ref[:, 3 * D_MODEL:4 * D_MODEL]
    scale2 = mod_ref[:, 4 * D_MODEL:5 * D_MODEL]
    x1 = x_ref[...] + gate1 * mix
    x1_ref[...] = x1
    ms = jnp.mean(x1 * x1, axis=-1, keepdims=True)
    h2 = x1 * lax.rsqrt(ms + EPS) * wn2_ref[...]
    h2_ref[...] = (h2 * (1.0 + scale2) + shift2).astype(BF16)


def _merge_call(o_n, glu, gates, x2d, mod3, w_attn_proj, w_conv_dw, ln_g, ln_b,
                w_conv_proj, w_out, w_norm2, seq, tm, per_seq_mod):
    n_tok = x2d.shape[0]
    tps = seq // tm
    mod_idx = (lambda i: (i // tps, 0, 0)) if per_seq_mod else (lambda i: (0, 0, 0))
    tok_spec = lambda w: pl.BlockSpec((tm, w), lambda i: (i, 0))
    in_specs = (
        [tok_spec(ATTN_W)] + _halo_specs(tm, CONV_CH, n_tok)
        + [tok_spec(2 * D_MODEL), tok_spec(D_MODEL),
           pl.BlockSpec((None, 1, N_MOD * D_MODEL), mod_idx),
           _const_spec((ATTN_W, D_MODEL)), _const_spec((DW_WIDTH, CONV_CH)),
           _const_spec((1, CONV_CH)), _const_spec((1, CONV_CH)),
           _const_spec((CONV_CH, D_MODEL)), _const_spec((D_MODEL, D_MODEL)),
           _const_spec((1, D_MODEL))])
    return pl.pallas_call(
        functools.partial(_merge_kernel, tm=tm, tps=tps),
        grid=(n_tok // tm,),
        in_specs=in_specs,
        out_specs=[tok_spec(D_MODEL), tok_spec(D_MODEL)],
        out_shape=[jax.ShapeDtypeStruct((n_tok, D_MODEL), F32),
                   jax.ShapeDtypeStruct((n_tok, D_MODEL), BF16)],
        scratch_shapes=[pltpu.VMEM((CONV_CH // LANES, tm + 2 * HALO, LANES), F32),
                        pltpu.VMEM((tm, CONV_CH), F32)],
        compiler_params=_params(1),
        name="merge",
    )(o_n, glu, glu, glu, gates, x2d, mod3, w_attn_proj, w_conv_dw, ln_g, ln_b,
      w_conv_proj, w_out, w_norm2)


def _ffn_kernel(h_ref, h_prev, h_next, x1_ref, mod_ref, wup_ref, wdw_ref, wdn_ref, wfn_ref,
                y_ref, lhs_ref, u_ref, acc_ref, *, tm, tps, cn):
    _fill_ext(lhs_ref, h_ref, h_prev, h_next, tm, tps)
    lhs = lhs_ref[...]
    n_slabs = cn // LANES

    def up_proj(par, half, c0):
        res = jnp.dot(lhs, wup_ref[:, c0:c0 + cn], preferred_element_type=F32)
        for s in range(n_slabs):
            u_ref[par, half, s] = res[:, s * LANES:(s + 1) * LANES]

    def conv(par, half, c0):
        outs = []
        for s in range(n_slabs):
            cols = slice(c0 + s * LANES, c0 + (s + 1) * LANES)
            taps = [u_ref[par, half, s, pl.ds(HALO - 1 + j, tm, stride=1), :] * wdw_ref[j:j + 1, cols]
                    for j in range(FFN_DW_WIDTH)]
            outs.append(taps[0] + taps[1] + taps[2])
        return jnp.concatenate(outs, axis=1)

    n_chunks = D_FF // cn
    depth = u_ref.shape[0] - 1
    for c in range(depth):
        up_proj(c, 0, c * cn)
        up_proj(c, 1, D_FF + c * cn)
    for c in range(n_chunks):
        a0 = c * cn
        b0 = D_FF + c * cn
        par = c % (depth + 1)
        if c + depth < n_chunks:
            nxt = (c + depth) % (depth + 1)
            up_proj(nxt, 0, a0 + depth * cn)
            up_proj(nxt, 1, b0 + depth * cn)
        a = conv(par, 0, a0)
        b = conv(par, 1, b0)
        act = (a * _sigmoid(a) * b).astype(BF16)
        down = jnp.dot(act, wdn_ref[a0:a0 + cn, :], preferred_element_type=F32)
        if c == 0:
            acc_ref[...] = down
        else:
            acc_ref[...] += down

    gate2 = mod_ref[:, 5 * D_MODEL:6 * D_MODEL]
    y = x1_ref[...] + gate2 * acc_ref[...]
    ms = jnp.mean(y * y, axis=-1, keepdims=True)
    y_ref[...] = y * lax.rsqrt(ms + EPS) * wfn_ref[...]


def _ffn_call(h2, x1, mod3, w_up, w_ffn_dw, w_down, w_final_norm, seq, tm, per_seq_mod):
    n_tok = x1.shape[0]
    tps = seq // tm
    cn = 256
    mod_idx = (lambda i: (i // tps, 0, 0)) if per_seq_mod else (lambda i: (0, 0, 0))
    tok_spec = lambda w: pl.BlockSpec((tm, w), lambda i: (i, 0))
    in_specs = (
        _halo_specs(tm, D_MODEL, n_tok)
        + [tok_spec(D_MODEL), pl.BlockSpec((None, 1, N_MOD * D_MODEL), mod_idx),
           _const_spec((D_MODEL, 2 * D_FF)), _const_spec((FFN_DW_WIDTH, 2 * D_FF)),
           _const_spec((D_FF, D_MODEL)), _const_spec((1, D_MODEL))])
    return pl.pallas_call(
        functools.partial(_ffn_kernel, tm=tm, tps=tps, cn=cn),
        grid=(n_tok // tm,),
        in_specs=in_specs,
        out_specs=tok_spec(D_MODEL),
        out_shape=jax.ShapeDtypeStruct((n_tok, D_MODEL), F32),
        scratch_shapes=[
            pltpu.VMEM((tm + 2 * HALO, D_MODEL), BF16),
            pltpu.VMEM((3, 2, cn // LANES, tm + 2 * HALO, LANES), F32),
            pltpu.VMEM((tm, D_MODEL), F32),
        ],
        compiler_params=_params(1),
        name="ffn",
    )(h2, h2, h2, x1, mod3, w_up, w_ffn_dw, w_down, w_final_norm)


def _tiles(seq):
    tm_in = min(seq, 1024)
    tm = min(seq, 512)
    tq = min(seq, 1024)
    tk = min(seq, 2048)
    heads = N_HEADS if seq <= 256 else 1
    return tm_in, tm, tq, tk, heads


def _trunk_group(x, mod3, per_seq_mod, use_rope, cache, lp):
    n_seq, seq, _ = x.shape
    tm_in, tm, tq, tk, heads = _tiles(seq)
    x2d = x.reshape(n_seq * seq, D_MODEL)
    emit_f32_kv = cache is None
    outs = _in_proj_call(x2d, mod3, lp["w_norm1"], lp["w_in"], seq, tm_in,
                         per_seq_mod, use_rope, emit_f32_kv)
    q, k, v, glu, gates = outs[:5]
    cache_k = cache_v = None
    if cache is not None:
        cache_k, cache_v = cache
    o_n = _attn_call(q, k, v, cache_k, cache_v, lp["lam"], lp["w_head_norm"],
                     n_seq, seq, tq, tk, heads)
    x1, h2 = _merge_call(o_n, glu, gates, x2d, mod3, lp["w_attn_proj"], lp["w_conv_dw"],
                         lp["conv_ln_g"], lp["conv_ln_b"], lp["w_conv_proj"], lp["w_out"],
                         lp["w_norm2"], seq, tm_in, per_seq_mod)
    y = _ffn_call(h2, x1, mod3, lp["w_up"], lp["w_ffn_dw"], lp["w_down"],
                  lp["w_final_norm"], seq, tm, per_seq_mod)
    return y.reshape(n_seq, seq, D_MODEL), outs[5:]


def kernel(x_prompt, x_sample, cache_k, cache_v, c, c_ctx, w_ada, b_ada, w_norm1, w_in, lambda_q1, lambda_k1, lambda_q2, lambda_k2, w_head_norm, w_attn_proj, w_conv_dw, conv_ln_g, conv_ln_b, w_conv_proj, w_out, w_norm2, w_up, w_ffn_dw, w_down, w_final_norm):
    assert w_in.shape[0] == 1, "single trunk layer"
    n_dec = x_sample.shape[0]
    n_ctx, seq_ctx = x_prompt.shape[0], x_prompt.shape[1]
    past = cache_k.shape[2]

    mod_rows = 16
    cc = jnp.concatenate(
        [c, c_ctx[None, :], jnp.zeros((mod_rows - n_dec - 1, D_MODEL), F32)], axis=0)
    mod = _mod_call(cc, w_ada[0], b_ada)
    mod_lat = mod[:n_dec].reshape(n_dec, 1, N_MOD * D_MODEL)
    mod_ctx = mod[n_dec:n_dec + 1].reshape(1, 1, N_MOD * D_MODEL)

    lp = dict(
        w_norm1=w_norm1, w_in=w_in[0].astype(BF16),
        lam=(lambda_q1, lambda_k1, lambda_q2, lambda_k2), w_head_norm=w_head_norm,
        w_attn_proj=w_attn_proj[0].astype(BF16), w_conv_dw=w_conv_dw[0],
        conv_ln_g=conv_ln_g, conv_ln_b=conv_ln_b,
        w_conv_proj=w_conv_proj[0].astype(BF16), w_out=w_out[0].astype(BF16),
        w_norm2=w_norm2, w_up=w_up[0].astype(BF16), w_ffn_dw=w_ffn_dw[0],
        w_down=w_down[0].astype(BF16), w_final_norm=w_final_norm[None, :])

    y_prompt, (kf, vf) = _trunk_group(x_prompt, mod_ctx, False, False, None, lp)
    cache = (cache_k[:, 0].astype(BF16).reshape(n_dec, past, ATTN_W),
             cache_v[:, 0].astype(BF16).reshape(n_dec, past, ATTN_W))
    y_sample, _ = _trunk_group(x_sample, mod_lat, True, True, cache, lp)
    new_k = kf.reshape(n_ctx, 1, seq_ctx, N_HEADS, HEAD_W)
    new_v = vf.reshape(n_ctx, 1, seq_ctx, N_HEADS, V_DIM)
    return (y_prompt, y_sample, new_k, new_v)
```
